```python
import functools
import jax, jax.numpy as jnp
from jax import lax
import numpy as np

D_MODEL = 1024
BATCH = 8
SEQ = 2048
DEPTH = 1
DEC_BATCH = 128
DEC_SEQ = 1
PAST_LEN = 8192
PAGE_SIZE = 128

ML_HEADS = 4
ML_HEAD_DIM = 256
ML_WIDTH = ML_HEADS * ML_HEAD_DIM
ML_CHUNK = 128
AT_HEADS = 8
AT_HEAD_DIM = 128
AT_WIDTH = AT_HEADS * AT_HEAD_DIM
MOBA_BLOCK = 256
MOBA_TOPK = 3
Q_BLOCK = 128
ROPE_THETA = 10000.0
D_FF = 2816
ALPHA = (2 * DEPTH) ** 0.25
BETA = (8 * DEPTH) ** -0.25
LN_EPS = 1e-5
NEG_INF = -1e30

OFF_QA = 0
OFF_KA = OFF_QA + ML_WIDTH
OFF_VA = OFF_KA + ML_WIDTH
OFF_OA = OFF_VA + ML_WIDTH
OFF_IF = OFF_OA + ML_WIDTH
OFF_QB = OFF_IF + 2 * ML_HEADS
OFF_KB = OFF_QB + AT_WIDTH
OFF_VB = OFF_KB + AT_WIDTH
OFF_G = OFF_VB + AT_WIDTH
IN_COLS = OFF_G + 2 * D_MODEL

kernel_name = 'hybrid_mlstm_moba_macaron_decode_step'

F32 = jnp.float32


def layer_norm(x, g, b):
    xf = x.astype(F32)
    mu = jnp.mean(xf, -1, keepdims=True)
    var = jnp.mean(jnp.square(xf - mu), -1, keepdims=True)
    return ((xf - mu) * lax.rsqrt(var + LN_EPS) * g + b).astype(x.dtype)


def swiglu(x, w_up, w_down):
    a, u = jnp.split(x @ w_up, 2, axis=-1)
    return (jax.nn.silu(a) * u) @ w_down


def heads(a, n):
    return a.reshape(a.shape[:-1] + (n, a.shape[-1] // n))


def head_norm(h, g):
    mu = jnp.mean(h, -1, keepdims=True)
    var = jnp.mean(jnp.square(h - mu), -1, keepdims=True)
    hn = (h - mu) * lax.rsqrt(var + LN_EPS)
    return hn.reshape(h.shape[:-2] + (-1,)) * g


def rope(x, pos):
    half = x.shape[-1] // 2
    inv = ROPE_THETA ** (-jnp.arange(half, dtype=F32) / half)
    ang = pos.astype(F32)[:, None] * inv[None, :]
    cos = jnp.cos(ang)[:, None, :]
    sin = jnp.sin(ang)[:, None, :]
    x1 = x[..., :half].astype(F32)
    x2 = x[..., half:].astype(F32)
    return jnp.concatenate([x1 * cos - x2 * sin, x2 * cos + x1 * sin], -1).astype(x.dtype)


def mlstm_chunkwise(q, k, v, i_pre, f_pre, C0, n0, m0):
    B, T, H, dh = q.shape
    L = ML_CHUNK if T % ML_CHUNK == 0 else T
    nc = T // L
    q = q.astype(F32)
    k = k.astype(F32) * (dh ** -0.5)
    v = v.astype(F32)
    ig = i_pre.astype(F32)
    logf = jax.nn.log_sigmoid(f_pre.astype(F32))

    def to_chunks(a):
        return a.reshape((B, nc, L) + a.shape[2:]).swapaxes(0, 1)

    causal = jnp.tril(jnp.ones((L, L), dtype=bool))

    def step(carry, xs):
        C, n, m = carry
        qc, kc, vc, ic, lfc = xs
        b = jnp.cumsum(lfc, axis=1)
        dlog = b[:, :, None, :] - b[:, None, :, :] + ic[:, None, :, :]
        dlog = jnp.where(causal[None, :, :, None], dlog, -jnp.inf)
        m_t = jnp.maximum(b + m[:, None, :], jnp.max(dlog, axis=2))
        w_intra = jnp.exp(dlog - m_t[:, :, None, :])
        w_inter = jnp.exp(b + m[:, None, :] - m_t)
        s = jnp.einsum('bthd,bshd->btsh', qc, kc) * w_intra
        num = jnp.einsum('btsh,bshd->bthd', s, vc) + w_inter[..., None] * jnp.einsum('bhvk,bthk->bthv', C, qc)
        den = jnp.sum(s, axis=2) + w_inter * jnp.einsum('bhk,bthk->bth', n, qc)
        h = num / jnp.maximum(jnp.abs(den), jnp.exp(-m_t))[..., None]
        m_new = m_t[:, -1]
        w_state = jnp.exp(b[:, -1:, :] - b + ic - m_new[:, None, :])
        decay = jnp.exp(b[:, -1] + m - m_new)
        C_new = decay[..., None, None] * C + jnp.einsum('bsh,bshv,bshk->bhvk', w_state, vc, kc)
        n_new = decay[..., None] * n + jnp.einsum('bsh,bshk->bhk', w_state, kc)
        return (C_new, n_new, m_new), h

    xs = (to_chunks(q), to_chunks(k), to_chunks(v), to_chunks(ig), to_chunks(logf))
    (C, n, m), hs = lax.scan(step, (C0.astype(F32), n0.astype(F32), m0.astype(F32)), xs)
    h = hs.swapaxes(0, 1).reshape(B, T, H, dh)
    return h, C, n, m


def moba_prompt(q, k, v):
    B, S, H, dh = q.shape
    nb = -(-S // MOBA_BLOCK)
    pad = nb * MOBA_BLOCK - S
    kb = jnp.pad(k, ((0, 0), (0, pad), (0, 0), (0, 0))).reshape(B, nb, MOBA_BLOCK, H, dh)
    vb = jnp.pad(v, ((0, 0), (0, pad), (0, 0), (0, 0))).reshape(B, nb, MOBA_BLOCK, H, dh)
    n_sel = min(MOBA_TOPK, nb - 1)
    scale = dh ** -0.5
    nq = S // Q_BLOCK
    hidx = jnp.arange(H)[None, :, None]
    if n_sel > 0:
        k_mean = jnp.mean(kb.astype(F32), axis=2)
        gate = jnp.einsum('bshd,bnhd->bshn', q.astype(F32), k_mean)
        q_blk = jnp.arange(S) // MOBA_BLOCK
        cand = jnp.arange(nb)[None, :] < q_blk[:, None]
        gate = jnp.where(cand[None, :, None, :], gate, NEG_INF)
        top_val, sel = lax.top_k(gate, n_sel)
        sel_ok = top_val > NEG_INF * 0.5

    def unit(idx):
        bi = idx // nq
        q0 = (idx % nq) * Q_BLOCK
        qc = lax.dynamic_slice_in_dim(q[bi], q0, Q_BLOCK, 0)
        qpos = q0 + jnp.arange(Q_BLOCK)
        own = q0 // MOBA_BLOCK
        k_own = lax.dynamic_index_in_dim(kb[bi], own, 0, keepdims=False)
        v_own = lax.dynamic_index_in_dim(vb[bi], own, 0, keepdims=False)
        kpos = own * MOBA_BLOCK + jnp.arange(MOBA_BLOCK)
        s_own = jnp.einsum('qhd,khd->qhk', qc, k_own, preferred_element_type=F32) * scale
        s_own = jnp.where((kpos[None, :] <= qpos[:, None])[:, None, :], s_own, NEG_INF)
        if n_sel == 0:
            p = jax.nn.softmax(s_own, axis=-1)
            out = jnp.einsum('qhk,khd->qhd', p, v_own.astype(F32))
        else:
            si = lax.dynamic_slice_in_dim(sel[bi], q0, Q_BLOCK, 0)
            ok = lax.dynamic_slice_in_dim(sel_ok[bi], q0, Q_BLOCK, 0)
            k_sel = kb[bi][si, :, hidx]
            v_sel = vb[bi][si, :, hidx]
            s_sel = jnp.einsum('qhd,qhnkd->qhnk', qc, k_sel, preferred_element_type=F32) * scale
            s_sel = jnp.where(ok[..., None], s_sel, NEG_INF).reshape(Q_BLOCK, H, n_sel * MOBA_BLOCK)
            v_sel = v_sel.reshape(Q_BLOCK, H, n_sel * MOBA_BLOCK, dh)
            p = jax.nn.softmax(jnp.concatenate([s_sel, s_own], -1), axis=-1)
            ns = n_sel * MOBA_BLOCK
            out = (jnp.einsum('qhk,qhkd->qhd', p[..., :ns], v_sel.astype(F32))
                   + jnp.einsum('qhk,khd->qhd', p[..., ns:], v_own.astype(F32)))
        return out.astype(q.dtype)

    out = lax.map(unit, jnp.arange(B * nq))
    return out.reshape(B, S, H * dh)


def moba_sample(q, k_new, v_new, pool_k, pool_v, page_table, layer):
    DB, T, H, dh = q.shape
    n_pages = PAST_LEN // PAGE_SIZE
    ppb = MOBA_BLOCK // PAGE_SIZE
    own = PAST_LEN // MOBA_BLOCK
    n_sel = min(MOBA_TOPK, own)
    n_tail = (n_pages - own * ppb) * PAGE_SIZE
    scale = dh ** -0.5
    tail_pages = page_table[:, own * ppb:]
    k_loc = jnp.concatenate([pool_k[layer, tail_pages].reshape(DB, n_tail, H, dh), k_new.astype(pool_k.dtype)], axis=1)
    v_loc = jnp.concatenate([pool_v[layer, tail_pages].reshape(DB, n_tail, H, dh), v_new.astype(pool_v.dtype)], axis=1)
    s_loc = jnp.einsum('bthd,blhd->bthl', q, k_loc, preferred_element_type=F32) * scale
    visible = jnp.arange(n_tail + T)[None, :] <= (n_tail + jnp.arange(T))[:, None]
    s_loc = jnp.where(visible[None, :, None, :], s_loc, NEG_INF)
    if n_sel == 0:
        p = jax.nn.softmax(s_loc, axis=-1)
        out = jnp.einsum('bthl,blhd->bthd', p, v_loc.astype(F32))
    else:
        past = page_table[:, :own * ppb]
        k_sum = jnp.sum(pool_k[layer, past], axis=2, dtype=F32)
        k_mean = jnp.sum(k_sum.reshape(DB, own, ppb, H, dh), axis=2) / MOBA_BLOCK
        gate = jnp.einsum('bthd,bnhd->bthn', q.astype(F32), k_mean)
        _, sel = lax.top_k(gate, n_sel)
        logical = sel[..., None] * ppb + jnp.arange(ppb)
        phys = page_table[jnp.arange(DB)[:, None, None, None, None], logical]
        hid = jnp.arange(H)[None, None, :, None, None]
        k_sel = pool_k[layer, phys, :, hid].reshape(DB, T, H, n_sel * MOBA_BLOCK, dh)
        v_sel = pool_v[layer, phys, :, hid].reshape(DB, T, H, n_sel * MOBA_BLOCK, dh)
        s_sel = jnp.einsum('bthd,bthkd->bthk', q, k_sel, preferred_element_type=F32) * scale
        p = jax.nn.softmax(jnp.concatenate([s_sel, s_loc], -1), axis=-1)
        ns = n_sel * MOBA_BLOCK
        out = (jnp.einsum('bthk,bthkd->bthd', p[..., :ns], v_sel.astype(F32))
               + jnp.einsum('bthl,blhd->bthd', p[..., ns:], v_loc.astype(F32)))
    return out.reshape(DB, T, H * dh).astype(q.dtype)


def decoder_layer(x, pos, attend, ml_state, ln_g, ln_b, ffn1_up, ffn1_down, ffn2_up, ffn2_down,
                  w_in, b_if, mh_g, w_a, w_b, w_o):
    h1 = layer_norm(ALPHA * x + 0.5 * swiglu(x, ffn1_up, ffn1_down), ln_g[0], ln_b[0])
    z = h1 @ w_in
    qa, ka, va, oa, gif, qb, kb, vb, gates = jnp.split(
        z, [OFF_KA, OFF_VA, OFF_OA, OFF_IF, OFF_QB, OFF_KB, OFF_VB, OFF_G], axis=-1)
    gif = gif + b_if
    ha, C, n, m = mlstm_chunkwise(heads(qa, ML_HEADS), heads(ka, ML_HEADS), heads(va, ML_HEADS),
                                  gif[..., :ML_HEADS], gif[..., ML_HEADS:], *ml_state)
    ha = (head_norm(ha, mh_g) * jax.nn.sigmoid(oa.astype(F32))).astype(x.dtype)
    qb = rope(heads(qb, AT_HEADS), pos)
    kb = rope(heads(kb, AT_HEADS), pos)
    vb = heads(vb, AT_HEADS)
    hb = attend(qb, kb, vb)
    g_a = jax.nn.sigmoid(gates[..., :D_MODEL])
    g_b = jax.nn.sigmoid(gates[..., D_MODEL:])
    mix = (g_a * (ha @ w_a) + g_b * (hb @ w_b)) @ w_o
    h2 = layer_norm(ALPHA * h1 + mix, ln_g[1], ln_b[1])
    y = layer_norm(ALPHA * h2 + 0.5 * swiglu(h2, ffn2_up, ffn2_down), ln_g[2], ln_b[2])
    return y, kb, vb, C, n, m


def setup_inputs(seed: int = 0) -> dict:
    key = jax.random.key(seed)
    ks = jax.random.split(key, 24)
    n_pages = PAST_LEN // PAGE_SIZE
    n_used = DEC_BATCH * n_pages
    n_phys = n_used + n_used // 4

    def nrm(k, shape, scale):
        return jax.random.normal(k, shape, F32) * scale

    b_i = nrm(ks[15], (DEPTH, ML_HEADS), 0.1)
    b_f = jnp.linspace(3.0, 6.0, ML_HEADS, dtype=F32)[None, :] + nrm(ks[16], (DEPTH, ML_HEADS), 0.1)
    return {
        'x_prompt': nrm(ks[0], (BATCH, SEQ, D_MODEL), 1.0),
        'x_sample': nrm(ks[1], (DEC_BATCH, DEC_SEQ, D_MODEL), 1.0),
        'cache_k': nrm(ks[2], (DEPTH, n_phys, PAGE_SIZE, AT_HEADS, AT_HEAD_DIM), 1.0),
        'cache_v': nrm(ks[3], (DEPTH, n_phys, PAGE_SIZE, AT_HEADS, AT_HEAD_DIM), 1.0),
        'state_mlstm_C': nrm(ks[4], (DEPTH, DEC_BATCH, ML_HEADS, ML_HEAD_DIM, ML_HEAD_DIM), 0.05),
        'state_mlstm_n': nrm(ks[5], (DEPTH, DEC_BATCH, ML_HEADS, ML_HEAD_DIM), 0.1),
        'state_mlstm_m': nrm(ks[6], (DEPTH, DEC_BATCH, ML_HEADS), 1.0),
        'page_table': jax.random.permutation(ks[7], n_phys)[:n_used].reshape(DEC_BATCH, n_pages).astype(jnp.int32),
        'ln_g': 1.0 + nrm(ks[8], (DEPTH, 3, D_MODEL), 0.02),
        'ln_b': nrm(ks[9], (DEPTH, 3, D_MODEL), 0.02),
        'ffn1_up': nrm(ks[10], (DEPTH, D_MODEL, 2 * D_FF), D_MODEL ** -0.5),
        'ffn1_down': nrm(ks[11], (DEPTH, D_FF, D_MODEL), BETA * D_FF ** -0.5),
        'ffn2_up': nrm(ks[12], (DEPTH, D_MODEL, 2 * D_FF), D_MODEL ** -0.5),
        'ffn2_down': nrm(ks[13], (DEPTH, D_FF, D_MODEL), BETA * D_FF ** -0.5),
        'w_in': nrm(ks[14], (DEPTH, D_MODEL, IN_COLS), D_MODEL ** -0.5),
        'b_if': jnp.concatenate([b_i, b_f], axis=-1),
        'mh_g': 1.0 + nrm(ks[17], (DEPTH, ML_WIDTH), 0.02),
        'w_a': nrm(ks[18], (DEPTH, ML_WIDTH, D_MODEL), ML_WIDTH ** -0.5),
        'w_b': nrm(ks[19], (DEPTH, AT_WIDTH, D_MODEL), AT_WIDTH ** -0.5),
        'w_o': nrm(ks[20], (DEPTH, D_MODEL, D_MODEL), BETA * D_MODEL ** -0.5),
    }


def reference(x_prompt, x_sample, cache_k, cache_v, state_mlstm_C, state_mlstm_n, state_mlstm_m, page_table,
              ln_g, ln_b, ffn1_up, ffn1_down, ffn2_up, ffn2_down, w_in, b_if, mh_g, w_a, w_b, w_o):
    B, S, _ = x_prompt.shape
    DB, T, _ = x_sample.shape
    pos_p = jnp.arange(S, dtype=jnp.int32)
    pos_s = PAST_LEN + jnp.arange(T, dtype=jnp.int32)
    yp, ys = x_prompt, x_sample
    kp, vp, Cp, np_, mp = [], [], [], [], []
    kss, vss, Cs, ns, ms = [], [], [], [], []
    for l in range(DEPTH):
        w = (ln_g[l], ln_b[l], ffn1_up[l], ffn1_down[l], ffn2_up[l], ffn2_down[l],
             w_in[l], b_if[l], mh_g[l], w_a[l], w_b[l], w_o[l])
        zero_state = (jnp.zeros((B, ML_HEADS, ML_HEAD_DIM, ML_HEAD_DIM), F32),
                      jnp.zeros((B, ML_HEADS, ML_HEAD_DIM), F32),
                      jnp.zeros((B, ML_HEADS), F32))
        yp, k1, v1, C1, n1, m1 = decoder_layer(yp, pos_p, moba_prompt, zero_state, *w)
        attend_s = functools.partial(moba_sample, pool_k=cache_k, pool_v=cache_v,
                                     page_table=page_table, layer=l)
        ys, k2, v2, C2, n2, m2 = decoder_layer(
            ys, pos_s, attend_s, (state_mlstm_C[l], state_mlstm_n[l], state_mlstm_m[l]), *w)
        kp.append(k1); vp.append(v1); Cp.append(C1); np_.append(n1); mp.append(m1)
        kss.append(k2); vss.append(v2); Cs.append(C2); ns.append(n2); ms.append(m2)
    k_prompt, v_prompt = jnp.stack(kp), jnp.stack(vp)
    C_prompt, n_prompt, m_prompt = jnp.stack(Cp), jnp.stack(np_), jnp.stack(mp)
    k_sample, v_sample = jnp.stack(kss), jnp.stack(vss)
    C_sample, n_sample, m_sample = jnp.stack(Cs), jnp.stack(ns), jnp.stack(ms)
    return (yp, ys, k_prompt, v_prompt, C_prompt, n_prompt, m_prompt,
            k_sample, v_sample, C_sample, n_sample, m_sample)
```

```python
import functools

import jax
import jax.numpy as jnp
from jax import lax
from jax.experimental import pallas as pl
from jax.experimental.pallas import tpu as pltpu

F32 = jnp.float32
BF16 = jnp.bfloat16

D_MODEL = 1024
PAST_LEN = 8192
PAGE_SIZE = 128
ML_HEADS = 4
ML_HEAD_DIM = 256
ML_WIDTH = ML_HEADS * ML_HEAD_DIM
ML_CHUNK = 128
AT_HEADS = 8
AT_HEAD_DIM = 128
AT_WIDTH = AT_HEADS * AT_HEAD_DIM
MOBA_BLOCK = 256
MOBA_TOPK = 3
ROPE_THETA = 10000.0
D_FF = 2816
LN_EPS = 1e-5
NEG_INF = -1e30

OFF_IF = 4 * ML_WIDTH
OFF_QB = OFF_IF + 2 * ML_HEADS
OFF_G = OFF_QB + 3 * AT_WIDTH

LANES = 128
VMEM_LIMIT_BYTES = 56 * 1024 * 1024
FF_CHUNK = 256
PAGES_PER_STEP = 8


def _params(*sem):
    return pltpu.CompilerParams(dimension_semantics=sem, vmem_limit_bytes=VMEM_LIMIT_BYTES)


def _const_spec(shape):
    return pl.BlockSpec(shape, lambda *_: (0,) * len(shape), pipeline_mode=pl.Buffered(1))


def _layer_norm(x, g, b):
    mu = jnp.mean(x, -1, keepdims=True)
    xc = x - mu
    var = jnp.mean(xc * xc, -1, keepdims=True)
    return xc * lax.rsqrt(var + LN_EPS) * g + b


def _split2(x):
    hi = x.astype(BF16)
    lo = (x - hi.astype(F32)).astype(BF16)
    return hi, lo


def _dot(a, b):
    return jnp.dot(a, b, preferred_element_type=F32)


def _dot_nt(a, b):
    return lax.dot_general(a, b, (((1,), (1,)), ((), ())), preferred_element_type=F32)


def _dot_tn(a, b):
    return lax.dot_general(a, b, (((0,), (0,)), ((), ())), preferred_element_type=F32)


def _ffn_ln_kernel(x_ref, wup_ref, wdn_ref, g_ref, b_ref, o_ref, h_ref, *, alpha):
    x = x_ref[...]
    xb = x.astype(BF16)
    for c in range(D_FF // FF_CHUNK):
        lo = c * FF_CHUNK
        a = _dot(xb, wup_ref[:, lo:lo + FF_CHUNK])
        u = _dot(xb, wup_ref[:, D_FF + lo:D_FF + lo + FF_CHUNK])
        h_ref[:, lo:lo + FF_CHUNK] = (a * jax.nn.sigmoid(a) * u).astype(BF16)
    y = _dot(h_ref[...], wdn_ref[...])
    o_ref[...] = _layer_norm(alpha * x + 0.5 * y, g_ref[...], b_ref[...])


def _ffn_ln(x, wup, wdn, g, b, *, alpha, tm):
    n, d = x.shape
    row = pl.BlockSpec((tm, d), lambda i: (i, 0))
    return pl.pallas_call(
        functools.partial(_ffn_ln_kernel, alpha=alpha),
        out_shape=jax.ShapeDtypeStruct((n, d), F32),
        grid=(n // tm,),
        in_specs=[row, _const_spec(wup.shape), _const_spec(wdn.shape),
                  _const_spec((1, d)), _const_spec((1, d))],
        out_specs=row,
        scratch_shapes=[pltpu.VMEM((tm, D_FF), BF16)],
        compiler_params=_params("parallel"),
        name="ffn_ln",
    )(x, wup, wdn, g, b)


def _proj_kernel(h_ref, w_ref, wif_hi_ref, wif_lo_ref, bif_ref, cos_ref, sin_ref,
                 qa_ref, ka_ref, va_ref, oa_ref, gif_ref, qb_ref, kb_ref, vb_ref):
    h = h_ref[...]
    h_hi, h_lo = _split2(h)

    def mm(block):
        return _dot(h_hi, w_ref[:, block * D_MODEL:(block + 1) * D_MODEL])

    qa_ref[...] = mm(0).astype(BF16)
    ka_ref[...] = (mm(1) * (ML_HEAD_DIM ** -0.5)).astype(BF16)
    va_ref[...] = mm(2).astype(BF16)
    oa_ref[...] = mm(3)
    wif_hi = wif_hi_ref[...]
    gif_ref[...] = (_dot(h_hi, wif_hi) + _dot(h_lo, wif_hi) + _dot(h_hi, wif_lo_ref[...])) + bif_ref[...]

    cos = cos_ref[...]
    sin = sin_ref[...]
    for block, out_ref in ((4, qb_ref), (5, kb_ref)):
        z = mm(block)
        for hd in range(AT_HEADS):
            zs = z[:, hd * AT_HEAD_DIM:(hd + 1) * AT_HEAD_DIM]
            out_ref[:, hd * AT_HEAD_DIM:(hd + 1) * AT_HEAD_DIM] = (
                zs * cos + pltpu.roll(zs, AT_HEAD_DIM // 2, 1) * sin)
    vb_ref[...] = mm(6)


def _proj(h, w_main, wif_hi, wif_lo, bif, cos, sin, *, tm):
    n, d = h.shape
    pos_blocks = cos.shape[0] // tm
    row = pl.BlockSpec((tm, d), lambda i: (i, 0))
    row_if = pl.BlockSpec((tm, LANES), lambda i: (i, 0))
    pos = pl.BlockSpec((tm, AT_HEAD_DIM), lambda i: (i % pos_blocks, 0))
    wide = lambda dt: jax.ShapeDtypeStruct((n, d), dt)
    return pl.pallas_call(
        _proj_kernel,
        out_shape=[wide(BF16), wide(BF16), wide(BF16), wide(F32),
                   jax.ShapeDtypeStruct((n, LANES), F32), wide(F32), wide(F32), wide(F32)],
        grid=(n // tm,),
        in_specs=[row, _const_spec(w_main.shape), _const_spec(wif_hi.shape), _const_spec(wif_lo.shape),
                  _const_spec((1, LANES)), pos, pos],
        out_specs=[row, row, row, row, row_if, row, row, row],
        compiler_params=_params("parallel"),
        name="proj",
    )(h, w_main, wif_hi, wif_lo, bif, cos, sin)


def _head_norm_gate(hh, gain, o_pre):
    mu = jnp.mean(hh, -1, keepdims=True)
    hc = hh - mu
    var = jnp.mean(hc * hc, -1, keepdims=True)
    return hc * lax.rsqrt(var + LN_EPS) * gain * jax.nn.sigmoid(o_pre)


def _mlstm_chunk_kernel(q_ref, k_ref, v_ref, oa_ref, gif_ref, mhg_ref,
                        ha_ref, c_ref, n_ref, m_ref):
    L = ML_CHUNK

    @pl.when(pl.program_id(1) == 0)
    def _():
        c_ref[...] = jnp.zeros_like(c_ref)
        n_ref[...] = jnp.zeros_like(n_ref)
        m_ref[...] = jnp.zeros_like(m_ref)

    gif = gif_ref[...]
    logf = jax.nn.log_sigmoid(gif)
    t_idx = lax.broadcasted_iota(jnp.int32, (L, L), 0)
    s_idx = lax.broadcasted_iota(jnp.int32, (L, L), 1)
    causal = s_idx <= t_idx
    tril = jnp.where(causal, 1.0, 0.0).astype(BF16)
    f_hi = logf.astype(BF16)
    f_rest = logf - f_hi.astype(F32)
    f_mid = f_rest.astype(BF16)
    f_lo = (f_rest - f_mid.astype(F32)).astype(BF16)
    b_col = _dot(tril, f_hi) + _dot(tril, f_mid) + _dot(tril, f_lo)
    b_row = b_col.T
    g_row = gif.T
    m_all = m_ref[0]
    lane = lax.broadcasted_iota(jnp.int32, m_all.shape, 1)

    for hd in range(ML_HEADS):
        hs = slice(hd * ML_HEAD_DIM, (hd + 1) * ML_HEAD_DIM)
        bc = b_col[:, ML_HEADS + hd:ML_HEADS + hd + 1]
        br = b_row[ML_HEADS + hd:ML_HEADS + hd + 1, :]
        ir = g_row[hd:hd + 1, :]
        ic = gif[:, hd:hd + 1]
        m_prev = m_all[:, hd:hd + 1]

        dlog = jnp.where(causal, bc - br + ir, -jnp.inf)
        m_t = jnp.maximum(bc + m_prev, jnp.max(dlog, axis=1, keepdims=True))
        w_intra = jnp.exp(dlog - m_t)
        w_inter = jnp.exp(bc + m_prev - m_t)

        qh = q_ref[:, hs]
        kh = k_ref[:, hs]
        vh = v_ref[:, hs]
        c_old = c_ref[0, hd]
        n_old = n_ref[0, hd:hd + 1, :]

        s = _dot_nt(qh, kh) * w_intra
        num = _dot(s.astype(BF16), vh) + w_inter * _dot_nt(qh, c_old.astype(BF16))
        qn = jnp.sum(qh.astype(F32) * n_old, axis=1, keepdims=True)
        den = jnp.sum(s, axis=1, keepdims=True) + w_inter * qn
        hh = num / jnp.maximum(jnp.abs(den), jnp.exp(-m_t))

        m_new = m_t[L - 1:L, :]
        b_last = bc[L - 1:L, :]
        w_state = jnp.exp(b_last - bc + ic - m_new)
        decay = jnp.exp(b_last + m_prev - m_new)
        vw = (vh.astype(F32) * w_state).astype(BF16)
        c_ref[0, hd] = decay * c_old + _dot_tn(vw, kh)
        n_ref[0, hd:hd + 1, :] = decay * n_old + jnp.sum(kh.astype(F32) * w_state, axis=0, keepdims=True)
        m_all = jnp.where(lane == hd, m_new, m_all)

        ha_ref[:, hs] = _head_norm_gate(hh, mhg_ref[:, hs], oa_ref[:, hs]).astype(BF16)

    m_ref[0] = m_all


def _mlstm_chunkwise(qa, ka, va, oa, gif, mhg, *, batch, seq):
    n, d = qa.shape
    nc = seq // ML_CHUNK
    row = pl.BlockSpec((ML_CHUNK, d), lambda b, c: (b * nc + c, 0))
    return pl.pallas_call(
        _mlstm_chunk_kernel,
        out_shape=[jax.ShapeDtypeStruct((n, d), BF16),
                   jax.ShapeDtypeStruct((batch, ML_HEADS, ML_HEAD_DIM, ML_HEAD_DIM), F32),
                   jax.ShapeDtypeStruct((batch, ML_HEADS, ML_HEAD_DIM), F32),
                   jax.ShapeDtypeStruct((batch, 1, LANES), F32)],
        grid=(batch, nc),
        in_specs=[row, row, row, row,
                  pl.BlockSpec((ML_CHUNK, LANES), lambda b, c: (b * nc + c, 0)),
                  _const_spec((1, d))],
        out_specs=[row,
                   pl.BlockSpec((1, ML_HEADS, ML_HEAD_DIM, ML_HEAD_DIM), lambda b, c: (b, 0, 0, 0)),
                   pl.BlockSpec((1, ML_HEADS, ML_HEAD_DIM), lambda b, c: (b, 0, 0)),
                   pl.BlockSpec((1, 1, LANES), lambda b, c: (b, 0, 0))],
        compiler_params=_params("parallel", "arbitrary"),
        name="mlstm_chunk",
    )(qa, ka, va, oa, gif, mhg)


def _mlstm_step_kernel(q_ref, k_ref, v_ref, oa_ref, gif_ref, mhg_ref, c_ref, n_ref, m_ref,
                       ha_ref, co_ref, no_ref, mo_ref):
    gif = gif_ref[0]
    m_all = m_ref[0]
    lane = lax.broadcasted_iota(jnp.int32, m_all.shape, 1)
    m_out = jnp.zeros_like(m_all)
    for hd in range(ML_HEADS):
        hs = slice(hd * ML_HEAD_DIM, (hd + 1) * ML_HEAD_DIM)
        q = q_ref[0, :, hs]
        k = k_ref[0, :, hs]
        v = v_ref[0, :, hs]
        ig = gif[:, hd:hd + 1]
        logf = jax.nn.log_sigmoid(gif[:, ML_HEADS + hd:ML_HEADS + hd + 1])
        m_prev = m_all[:, hd:hd + 1]
        m_t = jnp.maximum(logf + m_prev, ig)
        w_new = jnp.exp(ig - m_t)
        w_old = jnp.exp(logf + m_prev - m_t)
        c_old = c_ref[0, hd]
        n_old = n_ref[0, hd:hd + 1, :]
        s = jnp.sum(q * k, axis=1, keepdims=True) * w_new
        num = s * v + w_old * _dot_nt(q.astype(BF16), c_old.astype(BF16))
        den = s + w_old * jnp.sum(n_old * q, axis=1, keepdims=True)
        hh = num / jnp.maximum(jnp.abs(den), jnp.exp(-m_t))
        co_ref[0, hd] = w_old * c_old + _dot_tn((w_new * v).astype(BF16), k.astype(BF16))
        no_ref[0, hd:hd + 1, :] = w_old * n_old + w_new * k
        m_out = jnp.where(lane == hd, m_t, m_out)
        ha_ref[0, :, hs] = _head_norm_gate(hh, mhg_ref[:, hs], oa_ref[0, :, hs])
    mo_ref[0] = m_out


def _mlstm_step(q, k, v, oa, gif, mhg, c0, n0, m0):
    db, d = q.shape
    r3 = lambda a: a.astype(F32).reshape(db, 1, a.shape[-1])
    row = pl.BlockSpec((1, 1, d), lambda b: (b, 0, 0))
    row_if = pl.BlockSpec((1, 1, LANES), lambda b: (b, 0, 0))
    c_spec = pl.BlockSpec((1, ML_HEADS, ML_HEAD_DIM, ML_HEAD_DIM), lambda b: (b, 0, 0, 0))
    n_spec = pl.BlockSpec((1, ML_HEADS, ML_HEAD_DIM), lambda b: (b, 0, 0))
    m_spec = pl.BlockSpec((1, 1, ML_HEADS), lambda b: (b, 0, 0))
    ha, c, n, m = pl.pallas_call(
        _mlstm_step_kernel,
        out_shape=[jax.ShapeDtypeStruct((db, 1, d), F32),
                   jax.ShapeDtypeStruct(c0.shape, F32),
                   jax.ShapeDtypeStruct(n0.shape, F32),
                   jax.ShapeDtypeStruct((db, 1, ML_HEADS), F32)],
        grid=(db,),
        in_specs=[row, row, row, row, row_if, _const_spec((1, d)), c_spec, n_spec, m_spec],
        out_specs=[row, c_spec, n_spec, m_spec],
        compiler_params=_params("parallel"),
        name="mlstm_step",
    )(r3(q), r3(k), r3(v), r3(oa), r3(gif), mhg, c0, n0, m0.reshape(db, 1, ML_HEADS))
    return ha.reshape(db, d), c, n, m.reshape(db, ML_HEADS)


def _moba_prompt_kernel(q_ref, k_ref, v_ref, o_ref, kb_ref, vt_ref, kmh_ref, kml_ref, sel_ref, *, nb):
    i = pl.program_id(2)
    blk = MOBA_BLOCK

    @pl.when(i == 0)
    def _():
        means = []
        for j in range(nb):
            kj = k_ref[j * blk:(j + 1) * blk, :]
            kb_ref[j] = kj.astype(BF16)
            vt_ref[j] = v_ref[j * blk:(j + 1) * blk, :].T.astype(BF16)
            means.append(jnp.sum(kj, axis=0, keepdims=True) * (1.0 / blk))
        means.append(jnp.zeros((kmh_ref.shape[0] - nb, AT_HEAD_DIM), F32))
        hi, lo = _split2(jnp.concatenate(means, axis=0))
        kmh_ref[...] = hi
        kml_ref[...] = lo

    qt = q_ref[...].T
    qt_hi, qt_lo = _split2(qt)

    kmh = kmh_ref[...]
    gate = _dot(kmh, qt_hi) + _dot(kmh, qt_lo) + _dot(kml_ref[...], qt_hi)
    row = lax.broadcasted_iota(jnp.int32, gate.shape, 0)
    beaten = jnp.zeros(gate.shape, F32)
    for j in range(nb - 1):
        gj = gate[j:j + 1, :]
        wins = (gj > gate) | ((gj == gate) & (j < row))
        is_candidate = jnp.where(j < i, 1.0, 0.0)
        beaten = beaten + jnp.where(wins, is_candidate, 0.0)
    sel_ref[...] = jnp.where((beaten < MOBA_TOPK) & (row < i), 1.0, 0.0)

    scale = AT_HEAD_DIM ** -0.5
    kpos = lax.broadcasted_iota(jnp.int32, (blk, blk), 0)
    qpos = lax.broadcasted_iota(jnp.int32, (blk, blk), 1)

    st = jnp.where(kpos <= qpos, _dot(kb_ref[i], qt_hi) * scale, NEG_INF)
    m0 = jnp.max(st, axis=0, keepdims=True)
    p = jnp.exp(st - m0)
    l0 = jnp.sum(p, axis=0, keepdims=True)
    acc0 = _dot(vt_ref[i], p.astype(BF16))

    def past_block(j, carry):
        m, l, acc = carry
        st = jnp.where(sel_ref[pl.ds(j, 1), :] > 0.5, _dot(kb_ref[j], qt_hi) * scale, NEG_INF)
        m_new = jnp.maximum(m, jnp.max(st, axis=0, keepdims=True))
        alpha = jnp.exp(m - m_new)
        p = jnp.exp(st - m_new)
        return (m_new, alpha * l + jnp.sum(p, axis=0, keepdims=True),
                alpha * acc + _dot(vt_ref[j], p.astype(BF16)))

    _, l, acc = lax.fori_loop(0, i, past_block, (m0, l0, acc0))
    o_ref[...] = (acc / l).T.astype(BF16)


def _moba_prompt(qb, kb, vb, *, batch, seq):
    n, d = qb.shape
    nb = seq // MOBA_BLOCK
    gate_rows = 16
    assert seq % MOBA_BLOCK == 0 and nb <= gate_rows
    q_spec = pl.BlockSpec((MOBA_BLOCK, AT_HEAD_DIM), lambda b, h, i: (b * nb + i, h))
    kv_spec = pl.BlockSpec((seq, AT_HEAD_DIM), lambda b, h, i: (b, h))
    return pl.pallas_call(
        functools.partial(_moba_prompt_kernel, nb=nb),
        out_shape=jax.ShapeDtypeStruct((n, d), BF16),
        grid=(batch, AT_HEADS, nb),
        in_specs=[q_spec, kv_spec, kv_spec],
        out_specs=q_spec,
        scratch_shapes=[pltpu.VMEM((nb, MOBA_BLOCK, AT_HEAD_DIM), BF16),
                        pltpu.VMEM((nb, AT_HEAD_DIM, MOBA_BLOCK), BF16),
                        pltpu.VMEM((gate_rows, AT_HEAD_DIM), BF16),
                        pltpu.VMEM((gate_rows, AT_HEAD_DIM), BF16),
                        pltpu.VMEM((gate_rows, MOBA_BLOCK), F32)],
        compiler_params=_params("parallel", "parallel", "arbitrary"),
        name="moba_prompt",
    )(qb, kb, vb)


def _block_topk_kernel(pt_ref, q_ref, *refs, n_blocks):
    pages = refs[:PAGES_PER_STEP]
    sel_ref, gate_ref = refs[PAGES_PER_STEP:]
    g = pl.program_id(1)
    blocks_per_step = PAGES_PER_STEP * PAGE_SIZE // MOBA_BLOCK
    pages_per_block = MOBA_BLOCK // PAGE_SIZE

    @pl.when(g == 0)
    def _():
        gate_ref[...] = jnp.zeros_like(gate_ref)

    q = q_ref[0]
    lane = lax.broadcasted_iota(jnp.int32, gate_ref.shape, 1)
    gates = gate_ref[...]
    for j in range(blocks_per_step):
        ksum = pages[j * pages_per_block][...].sum(axis=0)
        for p in range(1, pages_per_block):
            ksum = ksum + pages[j * pages_per_block + p][...].sum(axis=0)
        gate = jnp.sum(q * (ksum * (1.0 / MOBA_BLOCK)), axis=1, keepdims=True)
        gates = jnp.where(lane == g * blocks_per_step + j, gate, gates)
    gate_ref[...] = gates

    @pl.when(g == pl.num_programs(1) - 1)
    def _():
        lane_f = lane.astype(F32)
        live = jnp.where(lane < n_blocks, gates, -jnp.inf)
        picks = jnp.zeros(gates.shape, F32)
        for t in range(MOBA_TOPK):
            best = jnp.max(live, axis=1, keepdims=True)
            idx = jnp.min(jnp.where(live == best, lane_f, float(n_blocks - 1)), axis=1, keepdims=True)
            picks = jnp.where(lane == t, idx, picks)
            live = jnp.where(lane_f == idx, -jnp.inf, live)
        sel_ref[0] = picks.astype(jnp.int32)


def _block_topk(page_table_flat, q3, cache_k, *, layer, n_pages, n_blocks):
    db = q3.shape[0]
    steps = n_pages // PAGES_PER_STEP

    def page_spec(j):
        return pl.BlockSpec((None, None, PAGE_SIZE, AT_HEADS, AT_HEAD_DIM),
                            lambda b, g, pt: (layer, pt[b * n_pages + g * PAGES_PER_STEP + j], 0, 0, 0))

    head_spec = pl.BlockSpec((1, AT_HEADS, AT_HEAD_DIM), lambda b, g, pt: (b, 0, 0))
    return pl.pallas_call(
        functools.partial(_block_topk_kernel, n_blocks=n_blocks),
        out_shape=jax.ShapeDtypeStruct((db, AT_HEADS, LANES), jnp.int32),
        grid_spec=pltpu.PrefetchScalarGridSpec(
            num_scalar_prefetch=1,
            grid=(db, steps),
            in_specs=[head_spec] + [page_spec(j) for j in range(PAGES_PER_STEP)],
            out_specs=pl.BlockSpec((1, AT_HEADS, LANES), lambda b, g, pt: (b, 0, 0)),
            scratch_shapes=[pltpu.VMEM((AT_HEADS, LANES), F32)]),
        compiler_params=_params("parallel", "arbitrary"),
        name="block_topk",
    )(page_table_flat, q3, *([cache_k] * PAGES_PER_STEP))


def _gather_attend_kernel(pt_ref, sel_ref, q_ref, kn_ref, vn_ref, ck_ref, cv_ref, o_ref,
                          kbuf, vbuf, sem, *, layer, n_pages):
    b = pl.program_id(0)
    nseq = pl.num_programs(0)
    pages_per_block = MOBA_BLOCK // PAGE_SIZE

    def copies(seq_id, slot):
        out = []
        for hd in range(AT_HEADS):
            for t in range(MOBA_TOPK):
                block = sel_ref[(seq_id * AT_HEADS + hd) * MOBA_TOPK + t]
                for p in range(pages_per_block):
                    phys = pt_ref[seq_id * n_pages + block * pages_per_block + p]
                    rows = pl.ds((t * pages_per_block + p) * PAGE_SIZE, PAGE_SIZE)
                    out.append(pltpu.make_async_copy(
                        ck_ref.at[layer, phys, :, hd, :], kbuf.at[slot, hd, rows, :], sem.at[0, slot]))
                    out.append(pltpu.make_async_copy(
                        cv_ref.at[layer, phys, :, hd, :], vbuf.at[slot, hd, rows, :], sem.at[1, slot]))
        return out

    @pl.when(b == 0)
    def _():
        for c in copies(0, 0):
            c.start()

    @pl.when(b + 1 < nseq)
    def _():
        for c in copies(b + 1, (b + 1) % 2):
            c.start()

    slot = b % 2
    for c in copies(b, slot):
        c.wait()

    scale = AT_HEAD_DIM ** -0.5
    q = q_ref[0]
    q_bf = q.astype(BF16)
    vn = vn_ref[0]
    s_new = jnp.sum(q * kn_ref[0], axis=1, keepdims=True) * scale
    row = lax.broadcasted_iota(jnp.int32, q.shape, 0)
    out = jnp.zeros(q.shape, F32)
    for hd in range(AT_HEADS):
        s = _dot_nt(q_bf, kbuf[slot, hd].astype(BF16)) * scale
        m = jnp.maximum(jnp.max(s, axis=1, keepdims=True), s_new)
        p = jnp.exp(s - m)
        p_new = jnp.exp(s_new - m)
        l = jnp.sum(p, axis=1, keepdims=True) + p_new
        o = (_dot(p.astype(BF16), vbuf[slot, hd].astype(BF16)) + p_new * vn) / l
        out = jnp.where(row == hd, o, out)
    o_ref[0] = out


def _gather_attend(page_table_flat, sel_flat, q3, kn3, vn3, cache_k, cache_v, *, layer, n_pages):
    db = q3.shape[0]
    rows = MOBA_TOPK * MOBA_BLOCK
    head_spec = pl.BlockSpec((1, AT_HEADS, AT_HEAD_DIM), lambda b, pt, sel: (b, 0, 0))
    any_spec = pl.BlockSpec(memory_space=pl.ANY)
    return pl.pallas_call(
        functools.partial(_gather_attend_kernel, layer=layer, n_pages=n_pages),
        out_shape=jax.ShapeDtypeStruct((db, AT_HEADS, AT_HEAD_DIM), F32),
        grid_spec=pltpu.PrefetchScalarGridSpec(
            num_scalar_prefetch=2,
            grid=(db,),
            in_specs=[head_spec, head_spec, head_spec, any_spec, any_spec],
            out_specs=head_spec,
            scratch_shapes=[pltpu.VMEM((2, AT_HEADS, rows, AT_HEAD_DIM), F32),
                            pltpu.VMEM((2, AT_HEADS, rows, AT_HEAD_DIM), F32),
                            pltpu.SemaphoreType.DMA((2, 2))]),
        compiler_params=_params("arbitrary"),
        name="gather_attend",
    )(page_table_flat, sel_flat, q3, kn3, vn3, cache_k, cache_v)


def _moba_sample(qb, kb, vb, cache_k, cache_v, page_table, *, layer):
    db = qb.shape[0]
    n_pages = PAST_LEN // PAGE_SIZE
    n_blocks = PAST_LEN // MOBA_BLOCK
    assert PAST_LEN % MOBA_BLOCK == 0 and n_blocks >= MOBA_TOPK and n_pages % PAGES_PER_STEP == 0
    assert (PAGES_PER_STEP * PAGE_SIZE) % MOBA_BLOCK == 0 and n_blocks <= LANES
    heads = lambda a: a.reshape(db, AT_HEADS, AT_HEAD_DIM)
    pt_flat = page_table.reshape(-1)
    q3 = heads(qb)
    sel = _block_topk(pt_flat, q3, cache_k, layer=layer, n_pages=n_pages, n_blocks=n_blocks)
    sel_flat = sel[:, :, :MOBA_TOPK].reshape(-1)
    out = _gather_attend(pt_flat, sel_flat, q3, heads(kb), heads(vb), cache_k, cache_v,
                         layer=layer, n_pages=n_pages)
    return out.reshape(db, AT_WIDTH)


def _mix_kernel(ha_ref, hb_ref, h1_ref, wg_ref, wa_ref, wb_ref, wo_ref, g_ref, b_ref, o_ref, *, alpha):
    h1 = h1_ref[...]
    gates = _dot(h1.astype(BF16), wg_ref[...])
    a = _dot(ha_ref[...], wa_ref[...])
    bm = _dot(hb_ref[...], wb_ref[...])
    inner = jax.nn.sigmoid(gates[:, :D_MODEL]) * a + jax.nn.sigmoid(gates[:, D_MODEL:]) * bm
    mix = _dot(inner.astype(BF16), wo_ref[...])
    o_ref[...] = _layer_norm(alpha * h1 + mix, g_ref[...], b_ref[...])


def _mix(ha, hb, h1, wg, wa, wb, wo, g, b, *, alpha, tm):
    n, d = h1.shape
    row = pl.BlockSpec((tm, d), lambda i: (i, 0))
    return pl.pallas_call(
        functools.partial(_mix_kernel, alpha=alpha),
        out_shape=jax.ShapeDtypeStruct((n, d), F32),
        grid=(n // tm,),
        in_specs=[row, row, row, _const_spec(wg.shape), _const_spec(wa.shape), _const_spec(wb.shape),
                  _const_spec(wo.shape), _const_spec((1, d)), _const_spec((1, d))],
        out_specs=row,
        compiler_params=_params("parallel"),
        name="mix",
    )(ha, hb, h1, wg, wa, wb, wo, g, b)


def _rope_tables(pos):
    half = AT_HEAD_DIM // 2
    inv = ROPE_THETA ** (-jnp.arange(half, dtype=F32) / half)
    ang = pos.astype(F32)[:, None] * inv[None, :]
    cos, sin = jnp.cos(ang), jnp.sin(ang)
    return jnp.concatenate([cos, cos], -1), jnp.concatenate([-sin, sin], -1)


def _layer_weights(l, ln_g, ln_b, ffn1_up, ffn1_down, ffn2_up, ffn2_down, w_in, b_if, mh_g, w_a, w_b, w_o):
    w = w_in[l]
    w_if = jnp.pad(w[:, OFF_IF:OFF_QB], ((0, 0), (0, LANES - 2 * ML_HEADS)))
    wif_hi, wif_lo = _split2(w_if)
    vec = lambda a: a.reshape(1, -1)
    return dict(
        ln_g=[vec(ln_g[l, i]) for i in range(3)], ln_b=[vec(ln_b[l, i]) for i in range(3)],
        up1=ffn1_up[l].astype(BF16), dn1=ffn1_down[l].astype(BF16),
        up2=ffn2_up[l].astype(BF16), dn2=ffn2_down[l].astype(BF16),
        w_main=jnp.concatenate([w[:, :OFF_IF], w[:, OFF_QB:OFF_G]], axis=1).astype(BF16),
        wif_hi=wif_hi, wif_lo=wif_lo,
        bif=jnp.pad(b_if[l], (0, LANES - 2 * ML_HEADS)).reshape(1, LANES),
        w_g=w[:, OFF_G:].astype(BF16), mhg=vec(mh_g[l]),
        w_a=w_a[l].astype(BF16), w_b=w_b[l].astype(BF16), w_o=w_o[l].astype(BF16))


def _decoder_layer(x, rope, mlstm_fn, moba_fn, w, *, alpha, tm):
    cos, sin = rope
    h1 = _ffn_ln(x, w["up1"], w["dn1"], w["ln_g"][0], w["ln_b"][0], alpha=alpha, tm=tm)
    qa, ka, va, oa, gif, qb, kb, vb = _proj(h1, w["w_main"], w["wif_hi"], w["wif_lo"], w["bif"],
                                            cos, sin, tm=min(tm, 256))
    ha, c, n, m = mlstm_fn(qa, ka, va, oa, gif, w["mhg"])
    hb = moba_fn(qb, kb, vb)
    h2 = _mix(ha.astype(BF16), hb.astype(BF16), h1, w["w_g"], w["w_a"], w["w_b"], w["w_o"],
              w["ln_g"][1], w["ln_b"][1], alpha=alpha, tm=tm)
    y = _ffn_ln(h2, w["up2"], w["dn2"], w["ln_g"][2], w["ln_b"][2], alpha=alpha, tm=tm)
    return y, kb, vb, c, n, m


def kernel(x_prompt, x_sample, cache_k, cache_v, state_mlstm_C, state_mlstm_n, state_mlstm_m, page_table,
           ln_g, ln_b, ffn1_up, ffn1_down, ffn2_up, ffn2_down, w_in, b_if, mh_g, w_a, w_b, w_o):
    batch, seq, d = x_prompt.shape
    db, t_new, _ = x_sample.shape
    depth = ln_g.shape[0]
    assert t_new == 1 and seq % ML_CHUNK == 0
    alpha = (2 * depth) ** 0.25

    rope_p = _rope_tables(jnp.arange(seq, dtype=jnp.int32))
    cos_s, sin_s = _rope_tables(PAST_LEN + jnp.arange(t_new, dtype=jnp.int32))
    rope_s = (jnp.tile(cos_s, (db, 1)), jnp.tile(sin_s, (db, 1)))

    yp = x_prompt.reshape(batch * seq, d)
    ys = x_sample.reshape(db * t_new, d)
    outs_p, outs_s = [], []
    for l in range(depth):
        w = _layer_weights(l, ln_g, ln_b, ffn1_up, ffn1_down, ffn2_up, ffn2_down, w_in, b_if, mh_g, w_a, w_b, w_o)

        yp, k1, v1, c1, n1, m1 = _decoder_layer(
            yp, rope_p,
            functools.partial(_mlstm_chunkwise, batch=batch, seq=seq),
            functools.partial(_moba_prompt, batch=batch, seq=seq),
            w, alpha=alpha, tm=512)
        outs_p.append((k1.reshape(batch, seq, AT_HEADS, AT_HEAD_DIM), v1.reshape(batch, seq, AT_HEADS, AT_HEAD_DIM),
                       c1, n1, m1[:, 0, :ML_HEADS]))

        ys, k2, v2, c2, n2, m2 = _decoder_layer(
            ys, rope_s,
            lambda qa, ka, va, oa, gif, mhg: _mlstm_step(
                qa, ka, va, oa, gif, mhg, state_mlstm_C[l], state_mlstm_n[l], state_mlstm_m[l]),
            functools.partial(_moba_sample, cache_k=cache_k, cache_v=cache_v, page_table=page_table, layer=l),
            w, alpha=alpha, tm=db * t_new)
        outs_s.append((k2.reshape(db, t_new, AT_HEADS, AT_HEAD_DIM), v2.reshape(db, t_new, AT_HEADS, AT_HEAD_DIM),
                       c2, n2, m2))

    kp, vp, cp, np_, mp = (jnp.stack(a) for a in zip(*outs_p))
    ks, vs, cs, ns, ms = (jnp.stack(a) for a in zip(*outs_s))
    return (yp.reshape(batch, seq, d), ys.reshape(db, t_new, d), kp, vp, cp, np_, mp, ks, vs, cs, ns, ms)
```

```python
import functools
from typing import NamedTuple

import jax
import jax.numpy as jnp
from jax import lax
from jax.experimental import pallas as pl
from jax.experimental.pallas import tpu as pltpu

F32 = jnp.float32
BF16 = jnp.bfloat16

D_MODEL = 1024
PAST_LEN = 8192
PAGE_SIZE = 128
ML_HEADS = 4
ML_HEAD_DIM = 256
ML_WIDTH = ML_HEADS * ML_HEAD_DIM
ML_CHUNK = 128
AT_HEADS = 8
AT_HEAD_DIM = 128
AT_WIDTH = AT_HEADS * AT_HEAD_DIM
MOBA_BLOCK = 256
MOBA_TOPK = 3
ROPE_THETA = 10000.0
D_FF = 2816
LN_EPS = 1e-5
NEG_INF = -1e30

OFF_IF = 4 * ML_WIDTH
OFF_QB = OFF_IF + 2 * ML_HEADS
OFF_G = OFF_QB + 3 * AT_WIDTH

LANES = 128
VMEM_LIMIT_BYTES = 56 * 1024 * 1024
FF_CHUNK = 256
PAGES_PER_STEP = 8
PAGES_PER_DENSE_STEP = 8
PAGES_PER_PROJ_STEP = 8
PAGES_PER_MLSTM_STEP = 8
PAGES_PER_MOBA_STEP = 8


def _params(*sem):
    return pltpu.CompilerParams(dimension_semantics=sem, vmem_limit_bytes=VMEM_LIMIT_BYTES)


def _const_spec(shape):
    return pl.BlockSpec(shape, lambda *_: (0,) * len(shape), pipeline_mode=pl.Buffered(1))


def _layer_norm(x, g, b):
    mu = jnp.mean(x, -1, keepdims=True)
    xc = x - mu
    var = jnp.mean(xc * xc, -1, keepdims=True)
    return xc * lax.rsqrt(var + LN_EPS) * g + b


def _split2(x):
    hi = x.astype(BF16)
    lo = (x - hi.astype(F32)).astype(BF16)
    return hi, lo


def _dot(a, b):
    return jnp.dot(a, b, preferred_element_type=F32)


def _dot_nt(a, b):
    return lax.dot_general(a, b, (((1,), (1,)), ((), ())), preferred_element_type=F32)


def _dot_tn(a, b):
    return lax.dot_general(a, b, (((0,), (0,)), ((), ())), preferred_element_type=F32)


class _PageStream(NamedTuple):
    page_table: jax.Array
    cache: jax.Array
    layer: int
    first_page: int
    pages_per_step: int


class _PageCursor:
    def __init__(self, page_table, cache, layer):
        self.page_table, self.cache, self.layer = page_table.reshape(-1), cache, layer
        self.next_page = 0

    def take(self, steps, pages_per_step):
        n = steps * pages_per_step
        if self.next_page + n > self.page_table.shape[0]:
            return None
        stream = _PageStream(self.page_table, self.cache, self.layer, self.next_page, pages_per_step)
        self.next_page += n
        return stream


def _sum_page_pairs(page_refs, ksum_ref):
    per_block = MOBA_BLOCK // PAGE_SIZE
    for blk in range(len(page_refs) // per_block):
        acc = page_refs[blk * per_block][...].sum(axis=0)
        for p in range(1, per_block):
            acc = acc + page_refs[blk * per_block + p][...].sum(axis=0)
        ksum_ref[blk] = acc


def _no_body():
    pass


def _tiled_call(body, args, *, name, steps, in_specs, out_specs, out_shape, scratch_shapes=(), stream=None,
                semantics="parallel"):
    if stream is None:
        outs = pl.pallas_call(
            body, out_shape=list(out_shape), grid=(steps,), in_specs=list(in_specs), out_specs=list(out_specs),
            scratch_shapes=list(scratch_shapes), compiler_params=_params(semantics), name=name)(*args)
        return list(outs), None

    pps = stream.pages_per_step
    bps = pps * PAGE_SIZE // MOBA_BLOCK
    assert (pps * PAGE_SIZE) % MOBA_BLOCK == 0
    n_in, n_out = len(in_specs), len(out_specs)

    def body_with_sums(pt_ref, *refs):
        del pt_ref
        ins, pages = refs[:n_in], refs[n_in:n_in + pps]
        outs, ksum_ref = refs[n_in + pps:n_in + pps + n_out], refs[n_in + pps + n_out]
        _sum_page_pairs(pages, ksum_ref)
        body(*ins, *outs, *refs[n_in + pps + n_out + 1:])

    def page_spec(j):
        return pl.BlockSpec((None, None, PAGE_SIZE, AT_HEADS, AT_HEAD_DIM),
                            lambda i, pt: (stream.layer, pt[stream.first_page + i * pps + j], 0, 0, 0))

    outs = pl.pallas_call(
        body_with_sums,
        out_shape=list(out_shape) + [jax.ShapeDtypeStruct((steps * bps, AT_HEADS, AT_HEAD_DIM), F32)],
        grid_spec=pltpu.PrefetchScalarGridSpec(
            num_scalar_prefetch=1, grid=(steps,),
            in_specs=list(in_specs) + [page_spec(j) for j in range(pps)],
            out_specs=list(out_specs) + [pl.BlockSpec((bps, AT_HEADS, AT_HEAD_DIM), lambda i, pt: (i, 0, 0))],
            scratch_shapes=list(scratch_shapes)),
        compiler_params=_params(semantics), name=name,
    )(stream.page_table, *args, *([stream.cache] * pps))
    return list(outs[:-1]), outs[-1]


def _ffn_ln_kernel(x_ref, wup_ref, wdn_ref, g_ref, b_ref, o_ref, h_ref, *, alpha):
    x = x_ref[...]
    xb = x.astype(BF16)
    for c in range(D_FF // FF_CHUNK):
        lo = c * FF_CHUNK
        a = _dot(xb, wup_ref[:, lo:lo + FF_CHUNK])
        u = _dot(xb, wup_ref[:, D_FF + lo:D_FF + lo + FF_CHUNK])
        h_ref[:, lo:lo + FF_CHUNK] = (a * jax.nn.sigmoid(a) * u).astype(BF16)
    y = _dot(h_ref[...], wdn_ref[...])
    o_ref[...] = _layer_norm(alpha * x + 0.5 * y, g_ref[...], b_ref[...])


def _ffn_ln(x, wup, wdn, g, b, *, alpha, tm, stream=None):
    n, d = x.shape
    row = pl.BlockSpec((tm, d), lambda i, *_: (i, 0))
    (y,), ksum = _tiled_call(
        functools.partial(_ffn_ln_kernel, alpha=alpha), (x, wup, wdn, g, b),
        name="ffn_ln", steps=n // tm,
        in_specs=[row, _const_spec(wup.shape), _const_spec(wdn.shape), _const_spec((1, d)), _const_spec((1, d))],
        out_specs=[row], out_shape=[jax.ShapeDtypeStruct((n, d), F32)],
        scratch_shapes=[pltpu.VMEM((tm, D_FF), BF16)], stream=stream)
    return y, ksum


def _proj_kernel(h_ref, w_ref, wif_hi_ref, wif_lo_ref, bif_ref, cos_ref, sin_ref,
                 qa_ref, ka_ref, va_ref, oa_ref, gif_ref, qb_ref, kb_ref, vb_ref):
    h = h_ref[...]
    h_hi, h_lo = _split2(h)

    def mm(block):
        return _dot(h_hi, w_ref[:, block * D_MODEL:(block + 1) * D_MODEL])

    qa_ref[...] = mm(0).astype(BF16)
    ka_ref[...] = (mm(1) * (ML_HEAD_DIM ** -0.5)).astype(BF16)
    va_ref[...] = mm(2).astype(BF16)
    oa_ref[...] = mm(3)
    wif_hi = wif_hi_ref[...]
    gif_ref[...] = (_dot(h_hi, wif_hi) + _dot(h_lo, wif_hi) + _dot(h_hi, wif_lo_ref[...])) + bif_ref[...]

    cos = cos_ref[...]
    sin = sin_ref[...]
    for block, out_ref in ((4, qb_ref), (5, kb_ref)):
        z = mm(block)
        for hd in range(AT_HEADS):
            zs = z[:, hd * AT_HEAD_DIM:(hd + 1) * AT_HEAD_DIM]
            out_ref[:, hd * AT_HEAD_DIM:(hd + 1) * AT_HEAD_DIM] = (
                zs * cos + pltpu.roll(zs, AT_HEAD_DIM // 2, 1) * sin)
    vb_ref[...] = mm(6)


def _proj(h, w_main, wif_hi, wif_lo, bif, cos, sin, *, tm, stream=None):
    n, d = h.shape
    pos_blocks = cos.shape[0] // tm
    row = pl.BlockSpec((tm, d), lambda i, *_: (i, 0))
    row_if = pl.BlockSpec((tm, LANES), lambda i, *_: (i, 0))
    pos = pl.BlockSpec((tm, AT_HEAD_DIM), lambda i, *_: (i % pos_blocks, 0))
    wide = lambda dt: jax.ShapeDtypeStruct((n, d), dt)
    return _tiled_call(
        _proj_kernel, (h, w_main, wif_hi, wif_lo, bif, cos, sin),
        name="proj", steps=n // tm,
        in_specs=[row, _const_spec(w_main.shape), _const_spec(wif_hi.shape), _const_spec(wif_lo.shape),
                  _const_spec((1, LANES)), pos, pos],
        out_specs=[row, row, row, row, row_if, row, row, row],
        out_shape=[wide(BF16), wide(BF16), wide(BF16), wide(F32),
                   jax.ShapeDtypeStruct((n, LANES), F32), wide(F32), wide(F32), wide(F32)],
        stream=stream)


def _head_norm_gate(hh, gain, o_pre):
    mu = jnp.mean(hh, -1, keepdims=True)
    hc = hh - mu
    var = jnp.mean(hc * hc, -1, keepdims=True)
    return hc * lax.rsqrt(var + LN_EPS) * gain * jax.nn.sigmoid(o_pre)


def _mlstm_chunk_kernel(q_ref, k_ref, v_ref, oa_ref, gif_ref, mhg_ref,
                        ha_ref, c_ref, n_ref, m_ref, *, chunks):
    L = ML_CHUNK

    @pl.when(pl.program_id(0) % chunks == 0)
    def _():
        c_ref[...] = jnp.zeros_like(c_ref)
        n_ref[...] = jnp.zeros_like(n_ref)
        m_ref[...] = jnp.zeros_like(m_ref)

    gif = gif_ref[...]
    logf = jax.nn.log_sigmoid(gif)
    t_idx = lax.broadcasted_iota(jnp.int32, (L, L), 0)
    s_idx = lax.broadcasted_iota(jnp.int32, (L, L), 1)
    causal = s_idx <= t_idx
    tril = jnp.where(causal, 1.0, 0.0).astype(BF16)
    f_hi = logf.astype(BF16)
    f_rest = logf - f_hi.astype(F32)
    f_mid = f_rest.astype(BF16)
    f_lo = (f_rest - f_mid.astype(F32)).astype(BF16)
    b_col = _dot(tril, f_hi) + _dot(tril, f_mid) + _dot(tril, f_lo)
    b_row = b_col.T
    g_row = gif.T
    m_all = m_ref[0]
    lane = lax.broadcasted_iota(jnp.int32, m_all.shape, 1)

    for hd in range(ML_HEADS):
        hs = slice(hd * ML_HEAD_DIM, (hd + 1) * ML_HEAD_DIM)
        bc = b_col[:, ML_HEADS + hd:ML_HEADS + hd + 1]
        br = b_row[ML_HEADS + hd:ML_HEADS + hd + 1, :]
        ir = g_row[hd:hd + 1, :]
        ic = gif[:, hd:hd + 1]
        m_prev = m_all[:, hd:hd + 1]

        dlog = jnp.where(causal, bc - br + ir, -jnp.inf)
        m_t = jnp.maximum(bc + m_prev, jnp.max(dlog, axis=1, keepdims=True))
        w_intra = jnp.exp(dlog - m_t)
        w_inter = jnp.exp(bc + m_prev - m_t)

        qh = q_ref[:, hs]
        kh = k_ref[:, hs]
        vh = v_ref[:, hs]
        c_old = c_ref[0, hd]
        n_old = n_ref[0, hd:hd + 1, :]

        s = _dot_nt(qh, kh) * w_intra
        num = _dot(s.astype(BF16), vh) + w_inter * _dot_nt(qh, c_old.astype(BF16))
        qn = jnp.sum(qh.astype(F32) * n_old, axis=1, keepdims=True)
        den = jnp.sum(s, axis=1, keepdims=True) + w_inter * qn
        hh = num / jnp.maximum(jnp.abs(den), jnp.exp(-m_t))

        m_new = m_t[L - 1:L, :]
        b_last = bc[L - 1:L, :]
        w_state = jnp.exp(b_last - bc + ic - m_new)
        decay = jnp.exp(b_last + m_prev - m_new)
        vw = (vh.astype(F32) * w_state).astype(BF16)
        c_ref[0, hd] = decay * c_old + _dot_tn(vw, kh)
        n_ref[0, hd:hd + 1, :] = decay * n_old + jnp.sum(kh.astype(F32) * w_state, axis=0, keepdims=True)
        m_all = jnp.where(lane == hd, m_new, m_all)

        ha_ref[:, hs] = _head_norm_gate(hh, mhg_ref[:, hs], oa_ref[:, hs]).astype(BF16)

    m_ref[0] = m_all


def _mlstm_chunkwise(qa, ka, va, oa, gif, mhg, *, batch, seq, cursor=None):
    n, d = qa.shape
    nc = seq // ML_CHUNK
    row = pl.BlockSpec((ML_CHUNK, d), lambda i, *_: (i, 0))
    stream = cursor.take(batch * nc, PAGES_PER_MLSTM_STEP) if cursor is not None else None
    (ha, c, n_state, m), ksum = _tiled_call(
        functools.partial(_mlstm_chunk_kernel, chunks=nc), (qa, ka, va, oa, gif, mhg),
        name="mlstm_chunk", steps=batch * nc, semantics="arbitrary",
        in_specs=[row, row, row, row, pl.BlockSpec((ML_CHUNK, LANES), lambda i, *_: (i, 0)), _const_spec((1, d))],
        out_specs=[row,
                   pl.BlockSpec((1, ML_HEADS, ML_HEAD_DIM, ML_HEAD_DIM), lambda i, *_: (i // nc, 0, 0, 0)),
                   pl.BlockSpec((1, ML_HEADS, ML_HEAD_DIM), lambda i, *_: (i // nc, 0, 0)),
                   pl.BlockSpec((1, 1, LANES), lambda i, *_: (i // nc, 0, 0))],
        out_shape=[jax.ShapeDtypeStruct((n, d), BF16),
                   jax.ShapeDtypeStruct((batch, ML_HEADS, ML_HEAD_DIM, ML_HEAD_DIM), F32),
                   jax.ShapeDtypeStruct((batch, ML_HEADS, ML_HEAD_DIM), F32),
                   jax.ShapeDtypeStruct((batch, 1, LANES), F32)],
        stream=stream)
    return (ha, c, n_state, m), ksum


def _mlstm_step_kernel(q_ref, k_ref, v_ref, oa_ref, gif_ref, mhg_ref, c_ref, n_ref, m_ref,
                       ha_ref, co_ref, no_ref, mo_ref):
    gif = gif_ref[0]
    m_all = m_ref[0]
    lane = lax.broadcasted_iota(jnp.int32, m_all.shape, 1)
    m_out = jnp.zeros_like(m_all)
    for hd in range(ML_HEADS):
        hs = slice(hd * ML_HEAD_DIM, (hd + 1) * ML_HEAD_DIM)
        q = q_ref[0, :, hs]
        k = k_ref[0, :, hs]
        v = v_ref[0, :, hs]
        ig = gif[:, hd:hd + 1]
        logf = jax.nn.log_sigmoid(gif[:, ML_HEADS + hd:ML_HEADS + hd + 1])
        m_prev = m_all[:, hd:hd + 1]
        m_t = jnp.maximum(logf + m_prev, ig)
        w_new = jnp.exp(ig - m_t)
        w_old = jnp.exp(logf + m_prev - m_t)
        c_old = c_ref[0, hd]
        n_old = n_ref[0, hd:hd + 1, :]
        s = jnp.sum(q * k, axis=1, keepdims=True) * w_new
        num = s * v + w_old * _dot_nt(q.astype(BF16), c_old.astype(BF16))
        den = s + w_old * jnp.sum(n_old * q, axis=1, keepdims=True)
        hh = num / jnp.maximum(jnp.abs(den), jnp.exp(-m_t))
        co_ref[0, hd] = w_old * c_old + _dot_tn((w_new * v).astype(BF16), k.astype(BF16))
        no_ref[0, hd:hd + 1, :] = w_old * n_old + w_new * k
        m_out = jnp.where(lane == hd, m_t, m_out)
        ha_ref[0, :, hs] = _head_norm_gate(hh, mhg_ref[:, hs], oa_ref[0, :, hs])
    mo_ref[0] = m_out


def _mlstm_step(q, k, v, oa, gif, mhg, c0, n0, m0):
    db, d = q.shape
    r3 = lambda a: a.astype(F32).reshape(db, 1, a.shape[-1])
    row = pl.BlockSpec((1, 1, d), lambda b: (b, 0, 0))
    row_if = pl.BlockSpec((1, 1, LANES), lambda b: (b, 0, 0))
    c_spec = pl.BlockSpec((1, ML_HEADS, ML_HEAD_DIM, ML_HEAD_DIM), lambda b: (b, 0, 0, 0))
    n_spec = pl.BlockSpec((1, ML_HEADS, ML_HEAD_DIM), lambda b: (b, 0, 0))
    m_spec = pl.BlockSpec((1, 1, ML_HEADS), lambda b: (b, 0, 0))
    ha, c, n, m = pl.pallas_call(
        _mlstm_step_kernel,
        out_shape=[jax.ShapeDtypeStruct((db, 1, d), F32),
                   jax.ShapeDtypeStruct(c0.shape, F32),
                   jax.ShapeDtypeStruct(n0.shape, F32),
                   jax.ShapeDtypeStruct((db, 1, ML_HEADS), F32)],
        grid=(db,),
        in_specs=[row, row, row, row, row_if, _const_spec((1, d)), c_spec, n_spec, m_spec],
        out_specs=[row, c_spec, n_spec, m_spec],
        compiler_params=_params("parallel"),
        name="mlstm_step",
    )(r3(q), r3(k), r3(v), r3(oa), r3(gif), mhg, c0, n0, m0.reshape(db, 1, ML_HEADS))
    return ha.reshape(db, d), c, n, m.reshape(db, ML_HEADS)


GATE_ROWS = 16


def _moba_prompt_kernel(q_ref, k_ref, v_ref, o_ref, kb_ref, vt_ref, s_ref, p_ref, *, nb):
    blk = MOBA_BLOCK
    means = []
    for j in range(nb):
        rows = slice(j * blk, (j + 1) * blk)
        kj = k_ref[rows, :]
        kb_ref[rows, :] = kj.astype(BF16)
        vt_ref[:, rows] = v_ref[rows, :].T.astype(BF16)
        means.append(jnp.sum(kj, axis=0, keepdims=True) * (1.0 / blk))
    means.append(jnp.zeros((GATE_ROWS - nb, AT_HEAD_DIM), F32))
    km_hi, km_lo = _split2(jnp.concatenate(means, axis=0))

    exp2_scale = (AT_HEAD_DIM ** -0.5) * 1.4426950408889634
    kpos = lax.broadcasted_iota(jnp.int32, (blk, blk), 0)
    qpos = lax.broadcasted_iota(jnp.int32, (blk, blk), 1)

    for i in range(nb):
        qt = q_ref[i * blk:(i + 1) * blk, :].T
        qt_hi = qt.astype(BF16)

        col_max = []
        for j in range(i + 1):
            rows = slice(j * blk, (j + 1) * blk)
            s = _dot(kb_ref[rows, :], qt_hi)
            if j == i:
                s = jnp.where(kpos <= qpos, s, NEG_INF)
            s_ref[rows, :] = s
            col_max.append(jnp.max(s, axis=0, keepdims=True))

        if i > MOBA_TOPK:
            qt_lo = (qt - qt_hi.astype(F32)).astype(BF16)
            gate = _dot(km_hi, qt_hi) + _dot(km_hi, qt_lo) + _dot(km_lo, qt_hi)
            row = lax.broadcasted_iota(jnp.int32, gate.shape, 0)
            beaten = jnp.zeros(gate.shape, F32)
            for j in range(i):
                gj = gate[j:j + 1, :]
                beaten = beaten + jnp.where((gj > gate) | ((gj == gate) & (j < row)), 1.0, 0.0)
            chosen = beaten < MOBA_TOPK
            sel = [chosen[j:j + 1, :] for j in range(i)]
            m = col_max[i]
            for j in range(i):
                m = jnp.maximum(m, jnp.where(sel[j], col_max[j], NEG_INF))
            offs = [jnp.where(sel[j], m, -NEG_INF) for j in range(i)] + [m]
        else:
            m = col_max[0]
            for cm in col_max[1:]:
                m = jnp.maximum(m, cm)
            offs = [m] * (i + 1)

        l = jnp.zeros_like(m)
        for j in range(i + 1):
            rows = slice(j * blk, (j + 1) * blk)
            p = jnp.exp2((s_ref[rows, :] - offs[j]) * exp2_scale)
            l = l + jnp.sum(p, axis=0, keepdims=True)
            p_ref[rows, :] = p.astype(BF16)
        n_keys = (i + 1) * blk
        acc = _dot(vt_ref[:, :n_keys], p_ref[:n_keys, :])
        o_ref[i * blk:(i + 1) * blk, :] = (acc / l).T.astype(BF16)


def _moba_prompt(qb, kb, vb, *, batch, seq, cursor=None):
    n, d = qb.shape
    nb = seq // MOBA_BLOCK
    assert seq % MOBA_BLOCK == 0 and nb <= GATE_ROWS
    spec = pl.BlockSpec((seq, AT_HEAD_DIM), lambda i, *_: (i // AT_HEADS, i % AT_HEADS))
    stream = cursor.take(batch * AT_HEADS, PAGES_PER_MOBA_STEP) if cursor is not None else None
    (hb,), ksum = _tiled_call(
        functools.partial(_moba_prompt_kernel, nb=nb), (qb, kb, vb),
        name="moba_prompt", steps=batch * AT_HEADS,
        in_specs=[spec, spec, spec], out_specs=[spec], out_shape=[jax.ShapeDtypeStruct((n, d), BF16)],
        scratch_shapes=[pltpu.VMEM((seq, AT_HEAD_DIM), BF16),
                        pltpu.VMEM((AT_HEAD_DIM, seq), BF16),
                        pltpu.VMEM((seq, MOBA_BLOCK), F32),
                        pltpu.VMEM((seq, MOBA_BLOCK), BF16)],
        stream=stream)
    return hb, ksum


def _gate_topk_kernel(q_ref, ksum_ref, sel_ref):
    n_blocks = ksum_ref.shape[1]
    q = q_ref[0]
    lane = lax.broadcasted_iota(jnp.int32, sel_ref.shape[1:], 1)
    gates = jnp.zeros(lane.shape, F32)
    for j in range(n_blocks):
        gate = jnp.sum(q * (ksum_ref[0, j] * (1.0 / MOBA_BLOCK)), axis=1, keepdims=True)
        gates = jnp.where(lane == j, gate, gates)

    lane_f = lane.astype(F32)
    live = jnp.where(lane < n_blocks, gates, -jnp.inf)
    picks = jnp.zeros(gates.shape, F32)
    for t in range(MOBA_TOPK):
        best = jnp.max(live, axis=1, keepdims=True)
        idx = jnp.min(jnp.where(live == best, lane_f, float(n_blocks - 1)), axis=1, keepdims=True)
        picks = jnp.where(lane == t, idx, picks)
        live = jnp.where(lane_f == idx, -jnp.inf, live)
    sel_ref[0] = picks.astype(jnp.int32)


def _gate_topk(q3, ksum):
    db, n_blocks = ksum.shape[:2]
    assert MOBA_TOPK <= n_blocks <= LANES
    return pl.pallas_call(
        _gate_topk_kernel,
        out_shape=jax.ShapeDtypeStruct((db, AT_HEADS, LANES), jnp.int32),
        grid=(db,),
        in_specs=[pl.BlockSpec((1, AT_HEADS, AT_HEAD_DIM), lambda b: (b, 0, 0)),
                  pl.BlockSpec((1, n_blocks, AT_HEADS, AT_HEAD_DIM), lambda b: (b, 0, 0, 0))],
        out_specs=pl.BlockSpec((1, AT_HEADS, LANES), lambda b: (b, 0, 0)),
        compiler_params=_params("parallel"),
        name="gate_topk",
    )(q3, ksum)


def _gather_attend_kernel(pt_ref, sel_ref, q_ref, kn_ref, vn_ref, ck_ref, cv_ref, o_ref,
                          kbuf, vbuf, sem, *, layer, n_pages):
    b = pl.program_id(0)
    nseq = pl.num_programs(0)
    pages_per_block = MOBA_BLOCK // PAGE_SIZE

    def copies(seq_id, slot):
        out = []
        for hd in range(AT_HEADS):
            for t in range(MOBA_TOPK):
                block = sel_ref[(seq_id * AT_HEADS + hd) * MOBA_TOPK + t]
                for p in range(pages_per_block):
                    phys = pt_ref[seq_id * n_pages + block * pages_per_block + p]
                    rows = pl.ds((t * pages_per_block + p) * PAGE_SIZE, PAGE_SIZE)
                    out.append(pltpu.make_async_copy(
                        ck_ref.at[layer, phys, :, hd, :], kbuf.at[slot, hd, rows, :], sem.at[0, slot]))
                    out.append(pltpu.make_async_copy(
                        cv_ref.at[layer, phys, :, hd, :], vbuf.at[slot, hd, rows, :], sem.at[1, slot]))
        return out

    @pl.when(b == 0)
    def _():
        for c in copies(0, 0):
            c.start()

    @pl.when(b + 1 < nseq)
    def _():
        for c in copies(b + 1, (b + 1) % 2):
            c.start()

    slot = b % 2
    for c in copies(b, slot):
        c.wait()

    scale = AT_HEAD_DIM ** -0.5
    q = q_ref[0]
    vn = vn_ref[0]
    s_new = jnp.sum(q * kn_ref[0], axis=1, keepdims=True) * scale
    row = lax.broadcasted_iota(jnp.int32, q.shape, 0)
    out = jnp.zeros(q.shape, F32)
    for hd in range(AT_HEADS):
        q_rep = jnp.broadcast_to(q[hd:hd + 1, :], (LANES, AT_HEAD_DIM)).astype(BF16)
        s = _dot_nt(kbuf[slot, hd].astype(BF16), q_rep) * scale
        s_own = s_new[hd:hd + 1, :]
        m = jnp.maximum(jnp.max(s, axis=0, keepdims=True), s_own)
        p = jnp.exp(s - m)
        p_new = jnp.exp(s_own - m)
        l = jnp.sum(p, axis=0, keepdims=True) + p_new
        o = (jnp.sum(p * vbuf[slot, hd], axis=0, keepdims=True) + p_new * vn[hd:hd + 1, :]) / l
        out = jnp.where(row == hd, o, out)
    o_ref[0] = out


def _gather_attend(page_table_flat, sel_flat, q3, kn3, vn3, cache_k, cache_v, *, layer, n_pages):
    db = q3.shape[0]
    rows = MOBA_TOPK * MOBA_BLOCK
    head_spec = pl.BlockSpec((1, AT_HEADS, AT_HEAD_DIM), lambda b, pt, sel: (b, 0, 0))
    any_spec = pl.BlockSpec(memory_space=pl.ANY)
    return pl.pallas_call(
        functools.partial(_gather_attend_kernel, layer=layer, n_pages=n_pages),
        out_shape=jax.ShapeDtypeStruct((db, AT_HEADS, AT_HEAD_DIM), F32),
        grid_spec=pltpu.PrefetchScalarGridSpec(
            num_scalar_prefetch=2,
            grid=(db,),
            in_specs=[head_spec, head_spec, head_spec, any_spec, any_spec],
            out_specs=head_spec,
            scratch_shapes=[pltpu.VMEM((2, AT_HEADS, rows, AT_HEAD_DIM), F32),
                            pltpu.VMEM((2, AT_HEADS, rows, AT_HEAD_DIM), F32),
                            pltpu.SemaphoreType.DMA((2, 2))]),
        compiler_params=_params("arbitrary"),
        name="gather_attend",
    )(page_table_flat, sel_flat, q3, kn3, vn3, cache_k, cache_v)


def _moba_sample(qb, kb, vb, cache_k, cache_v, page_table, cursor, block_sums, *, layer):
    db = qb.shape[0]
    n_pages = PAST_LEN // PAGE_SIZE
    n_blocks = PAST_LEN // MOBA_BLOCK
    assert PAST_LEN % MOBA_BLOCK == 0
    rest = db * n_pages - cursor.next_page
    assert rest % PAGES_PER_STEP == 0
    if rest:
        _, tail = _tiled_call(_no_body, (), name="block_sums", steps=rest // PAGES_PER_STEP,
                              in_specs=[], out_specs=[], out_shape=[],
                              stream=cursor.take(rest // PAGES_PER_STEP, PAGES_PER_STEP))
        block_sums = block_sums + [tail]
    ksum = jnp.concatenate(block_sums, axis=0).reshape(db, n_blocks, AT_HEADS, AT_HEAD_DIM)

    heads = lambda a: a.reshape(db, AT_HEADS, AT_HEAD_DIM)
    q3 = heads(qb)
    sel = _gate_topk(q3, ksum)
    sel_flat = sel[:, :, :MOBA_TOPK].reshape(-1)
    out = _gather_attend(page_table.reshape(-1), sel_flat, q3, heads(kb), heads(vb), cache_k, cache_v,
                         layer=layer, n_pages=n_pages)
    return out.reshape(db, AT_WIDTH)


def _mix_kernel(ha_ref, hb_ref, h1_ref, wg_ref, wa_ref, wb_ref, wo_ref, g_ref, b_ref, o_ref, *, alpha):
    h1 = h1_ref[...]
    gates = _dot(h1.astype(BF16), wg_ref[...])
    a = _dot(ha_ref[...], wa_ref[...])
    bm = _dot(hb_ref[...], wb_ref[...])
    inner = jax.nn.sigmoid(gates[:, :D_MODEL]) * a + jax.nn.sigmoid(gates[:, D_MODEL:]) * bm
    mix = _dot(inner.astype(BF16), wo_ref[...])
    o_ref[...] = _layer_norm(alpha * h1 + mix, g_ref[...], b_ref[...])


def _mix(ha, hb, h1, wg, wa, wb, wo, g, b, *, alpha, tm, stream=None):
    n, d = h1.shape
    row = pl.BlockSpec((tm, d), lambda i, *_: (i, 0))
    (h2,), ksum = _tiled_call(
        functools.partial(_mix_kernel, alpha=alpha), (ha, hb, h1, wg, wa, wb, wo, g, b),
        name="mix", steps=n // tm,
        in_specs=[row, row, row, _const_spec(wg.shape), _const_spec(wa.shape), _const_spec(wb.shape),
                  _const_spec(wo.shape), _const_spec((1, d)), _const_spec((1, d))],
        out_specs=[row], out_shape=[jax.ShapeDtypeStruct((n, d), F32)], stream=stream)
    return h2, ksum


def _rope_tables(pos):
    half = AT_HEAD_DIM // 2
    inv = ROPE_THETA ** (-jnp.arange(half, dtype=F32) / half)
    ang = pos.astype(F32)[:, None] * inv[None, :]
    cos, sin = jnp.cos(ang), jnp.sin(ang)
    return jnp.concatenate([cos, cos], -1), jnp.concatenate([-sin, sin], -1)


def _layer_weights(l, ln_g, ln_b, ffn1_up, ffn1_down, ffn2_up, ffn2_down, w_in, b_if, mh_g, w_a, w_b, w_o):
    w = w_in[l]
    w_if = jnp.pad(w[:, OFF_IF:OFF_QB], ((0, 0), (0, LANES - 2 * ML_HEADS)))
    wif_hi, wif_lo = _split2(w_if)
    vec = lambda a: a.reshape(1, -1)
    return dict(
        ln_g=[vec(ln_g[l, i]) for i in range(3)], ln_b=[vec(ln_b[l, i]) for i in range(3)],
        up1=ffn1_up[l].astype(BF16), dn1=ffn1_down[l].astype(BF16),
        up2=ffn2_up[l].astype(BF16), dn2=ffn2_down[l].astype(BF16),
        w_main=jnp.concatenate([w[:, :OFF_IF], w[:, OFF_QB:OFF_G]], axis=1).astype(BF16),
        wif_hi=wif_hi, wif_lo=wif_lo,
        bif=jnp.pad(b_if[l], (0, LANES - 2 * ML_HEADS)).reshape(1, LANES),
        w_g=w[:, OFF_G:].astype(BF16), mhg=vec(mh_g[l]),
        w_a=w_a[l].astype(BF16), w_b=w_b[l].astype(BF16), w_o=w_o[l].astype(BF16))


def _decoder_layer(x, rope, mlstm_fn, moba_fn, w, *, alpha, tm, cursor=None):
    cos, sin = rope
    n = x.shape[0]
    tm_proj = min(tm, 256)
    take = lambda steps, pages: cursor.take(steps, pages) if cursor is not None else None
    h1, s1 = _ffn_ln(x, w["up1"], w["dn1"], w["ln_g"][0], w["ln_b"][0], alpha=alpha, tm=tm,
                     stream=take(n // tm, PAGES_PER_DENSE_STEP))
    (qa, ka, va, oa, gif, qb, kb, vb), s2 = _proj(h1, w["w_main"], w["wif_hi"], w["wif_lo"], w["bif"],
                                                  cos, sin, tm=tm_proj, stream=take(n // tm_proj, PAGES_PER_PROJ_STEP))
    (ha, c, n_state, m), s3 = mlstm_fn(qa, ka, va, oa, gif, w["mhg"])
    hb, s4 = moba_fn(qb, kb, vb)
    h2, s5 = _mix(ha.astype(BF16), hb.astype(BF16), h1, w["w_g"], w["w_a"], w["w_b"], w["w_o"],
                  w["ln_g"][1], w["ln_b"][1], alpha=alpha, tm=tm, stream=take(n // tm, PAGES_PER_DENSE_STEP))
    y, s6 = _ffn_ln(h2, w["up2"], w["dn2"], w["ln_g"][2], w["ln_b"][2], alpha=alpha, tm=tm,
                    stream=take(n // tm, PAGES_PER_DENSE_STEP))
    return y, kb, vb, c, n_state, m, [s for s in (s1, s2, s3, s4, s5, s6) if s is not None]


def kernel(x_prompt, x_sample, cache_k, cache_v, state_mlstm_C, state_mlstm_n, state_mlstm_m, page_table,
           ln_g, ln_b, ffn1_up, ffn1_down, ffn2_up, ffn2_down, w_in, b_if, mh_g, w_a, w_b, w_o):
    batch, seq, d = x_prompt.shape
    db, t_new, _ = x_sample.shape
    depth = ln_g.shape[0]
    assert t_new == 1 and seq % ML_CHUNK == 0
    alpha = (2 * depth) ** 0.25

    rope_p = _rope_tables(jnp.arange(seq, dtype=jnp.int32))
    cos_s, sin_s = _rope_tables(PAST_LEN + jnp.arange(t_new, dtype=jnp.int32))
    rope_s = (jnp.tile(cos_s, (db, 1)), jnp.tile(sin_s, (db, 1)))

    yp = x_prompt.reshape(batch * seq, d)
    ys = x_sample.reshape(db * t_new, d)
    outs_p, outs_s = [], []
    for l in range(depth):
        w = _layer_weights(l, ln_g, ln_b, ffn1_up, ffn1_down, ffn2_up, ffn2_down, w_in, b_if, mh_g, w_a, w_b, w_o)

        cursor = _PageCursor(page_table, cache_k, l)
        yp, k1, v1, c1, n1, m1, block_sums = _decoder_layer(
            yp, rope_p,
            functools.partial(_mlstm_chunkwise, batch=batch, seq=seq, cursor=cursor),
            functools.partial(_moba_prompt, batch=batch, seq=seq, cursor=cursor),
            w, alpha=alpha, tm=512, cursor=cursor)
        outs_p.append((k1.reshape(batch, seq, AT_HEADS, AT_HEAD_DIM), v1.reshape(batch, seq, AT_HEADS, AT_HEAD_DIM),
                       c1, n1, m1[:, 0, :ML_HEADS]))

        ys, k2, v2, c2, n2, m2, _ = _decoder_layer(
            ys, rope_s,
            lambda qa, ka, va, oa, gif, mhg: (_mlstm_step(
                qa, ka, va, oa, gif, mhg, state_mlstm_C[l], state_mlstm_n[l], state_mlstm_m[l]), None),
            lambda qb, kb, vb: (_moba_sample(qb, kb, vb, cache_k, cache_v, page_table, cursor, block_sums,
                                             layer=l), None),
            w, alpha=alpha, tm=db * t_new)
        outs_s.append((k2.reshape(db, t_new, AT_HEADS, AT_HEAD_DIM), v2.reshape(db, t_new, AT_HEADS, AT_HEAD_DIM),
                       c2, n2, m2))

    kp, vp, cp, np_, mp = (jnp.stack(a) for a in zip(*outs_p))
    ks, vs, cs, ns, ms = (jnp.stack(a) for a in zip(*outs_s))
    return (yp.reshape(batch, seq, d), ys.reshape(db, t_new, d), kp, vp, cp, np_, mp, ks, vs, cs, ns, ms)
```

```python
import functools
from typing import NamedTuple

import jax
import jax.numpy as jnp
from jax import lax
from jax.experimental import pallas as pl
from jax.experimental.pallas import tpu as pltpu

F32 = jnp.float32
BF16 = jnp.bfloat16

D_MODEL = 1024
PAST_LEN = 8192
PAGE_SIZE = 128
ML_HEADS = 4
ML_HEAD_DIM = 256
ML_WIDTH = ML_HEADS * ML_HEAD_DIM
ML_CHUNK = 128
AT_HEADS = 8
AT_HEAD_DIM = 128
AT_WIDTH = AT_HEADS * AT_HEAD_DIM
MOBA_BLOCK = 256
MOBA_TOPK = 3
ROPE_THETA = 10000.0
D_FF = 2816
LN_EPS = 1e-5
NEG_INF = -1e30

OFF_IF = 4 * ML_WIDTH
OFF_QB = OFF_IF + 2 * ML_HEADS
OFF_G = OFF_QB + 3 * AT_WIDTH

LANES = 128
VMEM_LIMIT_BYTES = 56 * 1024 * 1024
FF_CHUNK = 256
PAGES_PER_STEP = 8
PAGES_PER_DENSE_STEP = 12
PAGES_PER_PROJ_STEP = 8
PAGES_PER_MLSTM_STEP = 12
PAGES_PER_MOBA_STEP = 12


def _params(*sem):
    return pltpu.CompilerParams(dimension_semantics=sem, vmem_limit_bytes=VMEM_LIMIT_BYTES)


def _const_spec(shape):
    return pl.BlockSpec(shape, lambda *_: (0,) * len(shape), pipeline_mode=pl.Buffered(1))


def _layer_norm(x, g, b):
    mu = jnp.mean(x, -1, keepdims=True)
    xc = x - mu
    var = jnp.mean(xc * xc, -1, keepdims=True)
    return xc * lax.rsqrt(var + LN_EPS) * g + b


def _split2(x):
    hi = x.astype(BF16)
    lo = (x - hi.astype(F32)).astype(BF16)
    return hi, lo


def _dot(a, b):
    return jnp.dot(a, b, preferred_element_type=F32)


def _dot_nt(a, b):
    return lax.dot_general(a, b, (((1,), (1,)), ((), ())), preferred_element_type=F32)


def _dot_tn(a, b):
    return lax.dot_general(a, b, (((0,), (0,)), ((), ())), preferred_element_type=F32)


class _PageStream(NamedTuple):
    page_table: jax.Array
    cache: jax.Array
    layer: int
    first_page: int
    pages_per_step: int


class _PageCursor:
    def __init__(self, page_table, cache, layer):
        self.page_table, self.cache, self.layer = page_table.reshape(-1), cache, layer
        self.next_page = 0

    def take(self, steps, pages_per_step):
        n = steps * pages_per_step
        if self.next_page + n > self.page_table.shape[0]:
            return None
        stream = _PageStream(self.page_table, self.cache, self.layer, self.next_page, pages_per_step)
        self.next_page += n
        return stream


def _sum_page_pairs(page_refs, ksum_ref):
    per_block = MOBA_BLOCK // PAGE_SIZE
    for blk in range(len(page_refs) // per_block):
        acc = page_refs[blk * per_block][...].sum(axis=0)
        for p in range(1, per_block):
            acc = acc + page_refs[blk * per_block + p][...].sum(axis=0)
        ksum_ref[blk] = acc


def _no_body():
    pass


def _tiled_call(body, args, *, name, steps, in_specs, out_specs, out_shape, scratch_shapes=(), stream=None,
                semantics="parallel"):
    if stream is None:
        outs = pl.pallas_call(
            body, out_shape=list(out_shape), grid=(steps,), in_specs=list(in_specs), out_specs=list(out_specs),
            scratch_shapes=list(scratch_shapes), compiler_params=_params(semantics), name=name)(*args)
        return list(outs), None

    pps = stream.pages_per_step
    bps = pps * PAGE_SIZE // MOBA_BLOCK
    assert (pps * PAGE_SIZE) % MOBA_BLOCK == 0
    n_in, n_out = len(in_specs), len(out_specs)

    def body_with_sums(pt_ref, *refs):
        del pt_ref
        ins, pages = refs[:n_in], refs[n_in:n_in + pps]
        outs, ksum_ref = refs[n_in + pps:n_in + pps + n_out], refs[n_in + pps + n_out]
        _sum_page_pairs(pages, ksum_ref)
        body(*ins, *outs, *refs[n_in + pps + n_out + 1:])

    def page_spec(j):
        return pl.BlockSpec((None, None, PAGE_SIZE, AT_HEADS, AT_HEAD_DIM),
                            lambda i, pt: (stream.layer, pt[stream.first_page + i * pps + j], 0, 0, 0))

    outs = pl.pallas_call(
        body_with_sums,
        out_shape=list(out_shape) + [jax.ShapeDtypeStruct((steps * bps, AT_HEADS, AT_HEAD_DIM), F32)],
        grid_spec=pltpu.PrefetchScalarGridSpec(
            num_scalar_prefetch=1, grid=(steps,),
            in_specs=list(in_specs) + [page_spec(j) for j in range(pps)],
            out_specs=list(out_specs) + [pl.BlockSpec((bps, AT_HEADS, AT_HEAD_DIM), lambda i, pt: (i, 0, 0))],
            scratch_shapes=list(scratch_shapes)),
        compiler_params=_params(semantics), name=name,
    )(stream.page_table, *args, *([stream.cache] * pps))
    return list(outs[:-1]), outs[-1]


def _ffn_ln_kernel(x_ref, wup_ref, wdn_ref, g_ref, b_ref, o_ref, h_ref, *, alpha):
    x = x_ref[...]
    xb = x.astype(BF16)
    for c in range(D_FF // FF_CHUNK):
        lo = c * FF_CHUNK
        a = _dot(xb, wup_ref[:, lo:lo + FF_CHUNK])
        u = _dot(xb, wup_ref[:, D_FF + lo:D_FF + lo + FF_CHUNK])
        h_ref[:, lo:lo + FF_CHUNK] = (a * jax.nn.sigmoid(a) * u).astype(BF16)
    y = _dot(h_ref[...], wdn_ref[...])
    o_ref[...] = _layer_norm(alpha * x + 0.5 * y, g_ref[...], b_ref[...])


def _ffn_ln(x, wup, wdn, g, b, *, alpha, tm, stream=None):
    n, d = x.shape
    row = pl.BlockSpec((tm, d), lambda i, *_: (i, 0))
    (y,), ksum = _tiled_call(
        functools.partial(_ffn_ln_kernel, alpha=alpha), (x, wup, wdn, g, b),
        name="ffn_ln", steps=n // tm,
        in_specs=[row, _const_spec(wup.shape), _const_spec(wdn.shape), _const_spec((1, d)), _const_spec((1, d))],
        out_specs=[row], out_shape=[jax.ShapeDtypeStruct((n, d), F32)],
        scratch_shapes=[pltpu.VMEM((tm, D_FF), BF16)], stream=stream)
    return y, ksum


def _proj_kernel(h_ref, w_ref, wif_hi_ref, wif_lo_ref, bif_ref, cos_ref, sin_ref,
                 qa_ref, ka_ref, va_ref, oa_ref, gif_ref, qb_ref, kb_ref, vb_ref):
    h = h_ref[...]
    h_hi, h_lo = _split2(h)

    def mm(block):
        return _dot(h_hi, w_ref[:, block * D_MODEL:(block + 1) * D_MODEL])

    qa_ref[...] = mm(0).astype(BF16)
    ka_ref[...] = (mm(1) * (ML_HEAD_DIM ** -0.5)).astype(BF16)
    va_ref[...] = mm(2).astype(BF16)
    oa_ref[...] = mm(3)
    wif_hi = wif_hi_ref[...]
    gif_ref[...] = (_dot(h_hi, wif_hi) + _dot(h_lo, wif_hi) + _dot(h_hi, wif_lo_ref[...])) + bif_ref[...]

    cos = cos_ref[...]
    sin = sin_ref[...]
    for block, out_ref in ((4, qb_ref), (5, kb_ref)):
        z = mm(block)
        for hd in range(AT_HEADS):
            zs = z[:, hd * AT_HEAD_DIM:(hd + 1) * AT_HEAD_DIM]
            out_ref[:, hd * AT_HEAD_DIM:(hd + 1) * AT_HEAD_DIM] = (
                zs * cos + pltpu.roll(zs, AT_HEAD_DIM // 2, 1) * sin)
    vb_ref[...] = mm(6)


def _proj(h, w_main, wif_hi, wif_lo, bif, cos, sin, *, tm, stream=None):
    n, d = h.shape
    pos_blocks = cos.shape[0] // tm
    row = pl.BlockSpec((tm, d), lambda i, *_: (i, 0))
    row_if = pl.BlockSpec((tm, LANES), lambda i, *_: (i, 0))
    pos = pl.BlockSpec((tm, AT_HEAD_DIM), lambda i, *_: (i % pos_blocks, 0))
    wide = lambda dt: jax.ShapeDtypeStruct((n, d), dt)
    return _tiled_call(
        _proj_kernel, (h, w_main, wif_hi, wif_lo, bif, cos, sin),
        name="proj", steps=n // tm,
        in_specs=[row, _const_spec(w_main.shape), _const_spec(wif_hi.shape), _const_spec(wif_lo.shape),
                  _const_spec((1, LANES)), pos, pos],
        out_specs=[row, row, row, row, row_if, row, row, row],
        out_shape=[wide(BF16), wide(BF16), wide(BF16), wide(F32),
                   jax.ShapeDtypeStruct((n, LANES), F32), wide(F32), wide(F32), wide(F32)],
        stream=stream)


def _head_norm_gate(hh, gain, o_pre):
    mu = jnp.mean(hh, -1, keepdims=True)
    hc = hh - mu
    var = jnp.mean(hc * hc, -1, keepdims=True)
    return hc * lax.rsqrt(var + LN_EPS) * gain * jax.nn.sigmoid(o_pre)


def _mlstm_chunk_kernel(q_ref, k_ref, v_ref, oa_ref, gif_ref, mhg_ref,
                        ha_ref, c_ref, n_ref, m_ref, *, chunks):
    L = ML_CHUNK

    @pl.when(pl.program_id(0) % chunks == 0)
    def _():
        c_ref[...] = jnp.zeros_like(c_ref)
        n_ref[...] = jnp.zeros_like(n_ref)
        m_ref[...] = jnp.zeros_like(m_ref)

    gif = gif_ref[...]
    logf = jax.nn.log_sigmoid(gif)
    t_idx = lax.broadcasted_iota(jnp.int32, (L, L), 0)
    s_idx = lax.broadcasted_iota(jnp.int32, (L, L), 1)
    causal = s_idx <= t_idx
    tril = jnp.where(causal, 1.0, 0.0).astype(BF16)
    f_hi = logf.astype(BF16)
    f_rest = logf - f_hi.astype(F32)
    f_mid = f_rest.astype(BF16)
    f_lo = (f_rest - f_mid.astype(F32)).astype(BF16)
    b_col = _dot(tril, f_hi) + _dot(tril, f_mid) + _dot(tril, f_lo)
    b_row = b_col.T
    g_row = gif.T
    m_all = m_ref[0]
    lane = lax.broadcasted_iota(jnp.int32, m_all.shape, 1)

    for hd in range(ML_HEADS):
        hs = slice(hd * ML_HEAD_DIM, (hd + 1) * ML_HEAD_DIM)
        bc = b_col[:, ML_HEADS + hd:ML_HEADS + hd + 1]
        br = b_row[ML_HEADS + hd:ML_HEADS + hd + 1, :]
        ir = g_row[hd:hd + 1, :]
        ic = gif[:, hd:hd + 1]
        m_prev = m_all[:, hd:hd + 1]

        dlog = jnp.where(causal, bc - br + ir, -jnp.inf)
        m_t = jnp.maximum(bc + m_prev, jnp.max(dlog, axis=1, keepdims=True))
        w_intra = jnp.exp(dlog - m_t)
        w_inter = jnp.exp(bc + m_prev - m_t)

        qh = q_ref[:, hs]
        kh = k_ref[:, hs]
        vh = v_ref[:, hs]
        c_old = c_ref[0, hd]
        n_old = n_ref[0, hd:hd + 1, :]

        s = _dot_nt(qh, kh) * w_intra
        num = _dot(s.astype(BF16), vh) + w_inter * _dot_nt(qh, c_old.astype(BF16))
        qn = jnp.sum(qh.astype(F32) * n_old, axis=1, keepdims=True)
        den = jnp.sum(s, axis=1, keepdims=True) + w_inter * qn
        hh = num / jnp.maximum(jnp.abs(den), jnp.exp(-m_t))

        m_new = m_t[L - 1:L, :]
        b_last = bc[L - 1:L, :]
        w_state = jnp.exp(b_last - bc + ic - m_new)
        decay = jnp.exp(b_last + m_prev - m_new)
        vw = (vh.astype(F32) * w_state).astype(BF16)
        c_ref[0, hd] = decay * c_old + _dot_tn(vw, kh)
        n_ref[0, hd:hd + 1, :] = decay * n_old + jnp.sum(kh.astype(F32) * w_state, axis=0, keepdims=True)
        m_all = jnp.where(lane == hd, m_new, m_all)

        ha_ref[:, hs] = _head_norm_gate(hh, mhg_ref[:, hs], oa_ref[:, hs]).astype(BF16)

    m_ref[0] = m_all


def _mlstm_chunkwise(qa, ka, va, oa, gif, mhg, *, batch, seq, cursor=None):
    n, d = qa.shape
    nc = seq // ML_CHUNK
    row = pl.BlockSpec((ML_CHUNK, d), lambda i, *_: (i, 0))
    stream = cursor.take(batch * nc, PAGES_PER_MLSTM_STEP) if cursor is not None else None
    (ha, c, n_state, m), ksum = _tiled_call(
        functools.partial(_mlstm_chunk_kernel, chunks=nc), (qa, ka, va, oa, gif, mhg),
        name="mlstm_chunk", steps=batch * nc, semantics="arbitrary",
        in_specs=[row, row, row, row, pl.BlockSpec((ML_CHUNK, LANES), lambda i, *_: (i, 0)), _const_spec((1, d))],
        out_specs=[row,
                   pl.BlockSpec((1, ML_HEADS, ML_HEAD_DIM, ML_HEAD_DIM), lambda i, *_: (i // nc, 0, 0, 0)),
                   pl.BlockSpec((1, ML_HEADS, ML_HEAD_DIM), lambda i, *_: (i // nc, 0, 0)),
                   pl.BlockSpec((1, 1, LANES), lambda i, *_: (i // nc, 0, 0))],
        out_shape=[jax.ShapeDtypeStruct((n, d), BF16),
                   jax.ShapeDtypeStruct((batch, ML_HEADS, ML_HEAD_DIM, ML_HEAD_DIM), F32),
                   jax.ShapeDtypeStruct((batch, ML_HEADS, ML_HEAD_DIM), F32),
                   jax.ShapeDtypeStruct((batch, 1, LANES), F32)],
        stream=stream)
    return (ha, c, n_state, m), ksum


def _mlstm_step_kernel(q_ref, k_ref, v_ref, oa_ref, gif_ref, mhg_ref, c_ref, n_ref, m_ref,
                       ha_ref, co_ref, no_ref, mo_ref):
    gif = gif_ref[0]
    m_all = m_ref[0]
    lane = lax.broadcasted_iota(jnp.int32, m_all.shape, 1)
    m_out = jnp.zeros_like(m_all)
    for hd in range(ML_HEADS):
        hs = slice(hd * ML_HEAD_DIM, (hd + 1) * ML_HEAD_DIM)
        q = q_ref[0, :, hs]
        k = k_ref[0, :, hs]
        v = v_ref[0, :, hs]
        ig = gif[:, hd:hd + 1]
        logf = jax.nn.log_sigmoid(gif[:, ML_HEADS + hd:ML_HEADS + hd + 1])
        m_prev = m_all[:, hd:hd + 1]
        m_t = jnp.maximum(logf + m_prev, ig)
        w_new = jnp.exp(ig - m_t)
        w_old = jnp.exp(logf + m_prev - m_t)
        c_old = c_ref[0, hd]
        n_old = n_ref[0, hd:hd + 1, :]
        s = jnp.sum(q * k, axis=1, keepdims=True) * w_new
        num = s * v + w_old * _dot_nt(q.astype(BF16), c_old.astype(BF16))
        den = s + w_old * jnp.sum(n_old * q, axis=1, keepdims=True)
        hh = num / jnp.maximum(jnp.abs(den), jnp.exp(-m_t))
        co_ref[0, hd] = w_old * c_old + _dot_tn((w_new * v).astype(BF16), k.astype(BF16))
        no_ref[0, hd:hd + 1, :] = w_old * n_old + w_new * k
        m_out = jnp.where(lane == hd, m_t, m_out)
        ha_ref[0, :, hs] = _head_norm_gate(hh, mhg_ref[:, hs], oa_ref[0, :, hs])
    mo_ref[0] = m_out


def _mlstm_step(q, k, v, oa, gif, mhg, c0, n0, m0):
    db, d = q.shape
    r3 = lambda a: a.astype(F32).reshape(db, 1, a.shape[-1])
    row = pl.BlockSpec((1, 1, d), lambda b: (b, 0, 0))
    row_if = pl.BlockSpec((1, 1, LANES), lambda b: (b, 0, 0))
    c_spec = pl.BlockSpec((1, ML_HEADS, ML_HEAD_DIM, ML_HEAD_DIM), lambda b: (b, 0, 0, 0))
    n_spec = pl.BlockSpec((1, ML_HEADS, ML_HEAD_DIM), lambda b: (b, 0, 0))
    m_spec = pl.BlockSpec((1, 1, ML_HEADS), lambda b: (b, 0, 0))
    ha, c, n, m = pl.pallas_call(
        _mlstm_step_kernel,
        out_shape=[jax.ShapeDtypeStruct((db, 1, d), F32),
                   jax.ShapeDtypeStruct(c0.shape, F32),
                   jax.ShapeDtypeStruct(n0.shape, F32),
                   jax.ShapeDtypeStruct((db, 1, ML_HEADS), F32)],
        grid=(db,),
        in_specs=[row, row, row, row, row_if, _const_spec((1, d)), c_spec, n_spec, m_spec],
        out_specs=[row, c_spec, n_spec, m_spec],
        compiler_params=_params("parallel"),
        name="mlstm_step",
    )(r3(q), r3(k), r3(v), r3(oa), r3(gif), mhg, c0, n0, m0.reshape(db, 1, ML_HEADS))
    return ha.reshape(db, d), c, n, m.reshape(db, ML_HEADS)


GATE_ROWS = 16


def _moba_prompt_kernel(q_ref, k_ref, v_ref, o_ref, kb_ref, vt_ref, s_ref, p_ref, *, nb):
    blk = MOBA_BLOCK
    means = []
    for j in range(nb):
        rows = slice(j * blk, (j + 1) * blk)
        kj = k_ref[rows, :]
        kb_ref[rows, :] = kj.astype(BF16)
        vt_ref[:, rows] = v_ref[rows, :].T.astype(BF16)
        means.append(jnp.sum(kj, axis=0, keepdims=True) * (1.0 / blk))
    means.append(jnp.zeros((GATE_ROWS - nb, AT_HEAD_DIM), F32))
    km_hi, km_lo = _split2(jnp.concatenate(means, axis=0))

    exp2_scale = (AT_HEAD_DIM ** -0.5) * 1.4426950408889634
    kpos = lax.broadcasted_iota(jnp.int32, (blk, blk), 0)
    qpos = lax.broadcasted_iota(jnp.int32, (blk, blk), 1)

    for i in range(nb):
        qt = q_ref[i * blk:(i + 1) * blk, :].T
        qt_hi = qt.astype(BF16)

        col_max = []
        for j in range(i + 1):
            rows = slice(j * blk, (j + 1) * blk)
            s = _dot(kb_ref[rows, :], qt_hi)
            if j == i:
                s = jnp.where(kpos <= qpos, s, NEG_INF)
            s_ref[rows, :] = s
            col_max.append(jnp.max(s, axis=0, keepdims=True))

        if i > MOBA_TOPK:
            qt_lo = (qt - qt_hi.astype(F32)).astype(BF16)
            gate = _dot(km_hi, qt_hi) + _dot(km_hi, qt_lo) + _dot(km_lo, qt_hi)
            row = lax.broadcasted_iota(jnp.int32, gate.shape, 0)
            beaten = jnp.zeros(gate.shape, F32)
            for j in range(i):
                gj = gate[j:j + 1, :]
                beaten = beaten + jnp.where((gj > gate) | ((gj == gate) & (j < row)), 1.0, 0.0)
            chosen = beaten < MOBA_TOPK
            sel = [chosen[j:j + 1, :] for j in range(i)]
            m = col_max[i]
            for j in range(i):
                m = jnp.maximum(m, jnp.where(sel[j], col_max[j], NEG_INF))
            offs = [jnp.where(sel[j], m, -NEG_INF) for j in range(i)] + [m]
        else:
            m = col_max[0]
            for cm in col_max[1:]:
                m = jnp.maximum(m, cm)
            offs = [m] * (i + 1)

        l = jnp.zeros_like(m)
        for j in range(i + 1):
            rows = slice(j * blk, (j + 1) * blk)
            p = jnp.exp2((s_ref[rows, :] - offs[j]) * exp2_scale)
            l = l + jnp.sum(p, axis=0, keepdims=True)
            p_ref[rows, :] = p.astype(BF16)
        n_keys = (i + 1) * blk
        acc = _dot(vt_ref[:, :n_keys], p_ref[:n_keys, :])
        o_ref[i * blk:(i + 1) * blk, :] = (acc / l).T.astype(BF16)


def _moba_prompt(qb, kb, vb, *, batch, seq, cursor=None):
    n, d = qb.shape
    nb = seq // MOBA_BLOCK
    assert seq % MOBA_BLOCK == 0 and nb <= GATE_ROWS
    spec = pl.BlockSpec((seq, AT_HEAD_DIM), lambda i, *_: (i // AT_HEADS, i % AT_HEADS))
    stream = cursor.take(batch * AT_HEADS, PAGES_PER_MOBA_STEP) if cursor is not None else None
    (hb,), ksum = _tiled_call(
        functools.partial(_moba_prompt_kernel, nb=nb), (qb, kb, vb),
        name="moba_prompt", steps=batch * AT_HEADS,
        in_specs=[spec, spec, spec], out_specs=[spec], out_shape=[jax.ShapeDtypeStruct((n, d), BF16)],
        scratch_shapes=[pltpu.VMEM((seq, AT_HEAD_DIM), BF16),
                        pltpu.VMEM((AT_HEAD_DIM, seq), BF16),
                        pltpu.VMEM((seq, MOBA_BLOCK), F32),
                        pltpu.VMEM((seq, MOBA_BLOCK), BF16)],
        stream=stream)
    return hb, ksum


def _gate_topk_kernel(q_ref, ksum_ref, sel_ref):
    n_seq, n_blocks = ksum_ref.shape[:2]
    lane = lax.broadcasted_iota(jnp.int32, sel_ref.shape[1:], 1)
    lane_f = lane.astype(F32)
    for s in range(n_seq):
        q = q_ref[s]
        gates = jnp.zeros(lane.shape, F32)
        for j in range(n_blocks):
            gate = jnp.sum(q * (ksum_ref[s, j] * (1.0 / MOBA_BLOCK)), axis=1, keepdims=True)
            gates = jnp.where(lane == j, gate, gates)

        live = jnp.where(lane < n_blocks, gates, -jnp.inf)
        picks = jnp.zeros(gates.shape, F32)
        for t in range(MOBA_TOPK):
            best = jnp.max(live, axis=1, keepdims=True)
            idx = jnp.min(jnp.where(live == best, lane_f, float(n_blocks - 1)), axis=1, keepdims=True)
            picks = jnp.where(lane == t, idx, picks)
            live = jnp.where(lane_f == idx, -jnp.inf, live)
        sel_ref[s] = picks.astype(jnp.int32)


def _gate_topk(q3, ksum):
    db, n_blocks = ksum.shape[:2]
    assert MOBA_TOPK <= n_blocks <= LANES
    group = 8 if db % 8 == 0 else 1
    return pl.pallas_call(
        _gate_topk_kernel,
        out_shape=jax.ShapeDtypeStruct((db, AT_HEADS, LANES), jnp.int32),
        grid=(db // group,),
        in_specs=[pl.BlockSpec((group, AT_HEADS, AT_HEAD_DIM), lambda b: (b, 0, 0)),
                  pl.BlockSpec((group, n_blocks, AT_HEADS, AT_HEAD_DIM), lambda b: (b, 0, 0, 0))],
        out_specs=pl.BlockSpec((group, AT_HEADS, LANES), lambda b: (b, 0, 0)),
        compiler_params=_params("parallel"),
        name="gate_topk",
    )(q3, ksum)


def _gather_attend_kernel(pt_ref, sel_ref, q_ref, kn_ref, vn_ref, ck_ref, cv_ref, o_ref,
                          kbuf, vbuf, sem, *, layer, n_pages):
    b = pl.program_id(0)
    nseq = pl.num_programs(0)
    pages_per_block = MOBA_BLOCK // PAGE_SIZE

    def copies(seq_id, slot):
        out = []
        for hd in range(AT_HEADS):
            for t in range(MOBA_TOPK):
                block = sel_ref[(seq_id * AT_HEADS + hd) * MOBA_TOPK + t]
                for p in range(pages_per_block):
                    phys = pt_ref[seq_id * n_pages + block * pages_per_block + p]
                    rows = pl.ds((t * pages_per_block + p) * PAGE_SIZE, PAGE_SIZE)
                    out.append(pltpu.make_async_copy(
                        ck_ref.at[layer, phys, :, hd, :], kbuf.at[slot, hd, rows, :], sem.at[0, slot]))
                    out.append(pltpu.make_async_copy(
                        cv_ref.at[layer, phys, :, hd, :], vbuf.at[slot, hd, rows, :], sem.at[1, slot]))
        return out

    @pl.when(b == 0)
    def _():
        for c in copies(0, 0):
            c.start()

    @pl.when(b + 1 < nseq)
    def _():
        for c in copies(b + 1, (b + 1) % 2):
            c.start()

    slot = b % 2
    for c in copies(b, slot):
        c.wait()

    scale = AT_HEAD_DIM ** -0.5
    q = q_ref[0]
    vn = vn_ref[0]
    s_new = jnp.sum(q * kn_ref[0], axis=1, keepdims=True) * scale
    row = lax.broadcasted_iota(jnp.int32, q.shape, 0)
    out = jnp.zeros(q.shape, F32)
    for hd in range(AT_HEADS):
        q_rep = jnp.broadcast_to(q[hd:hd + 1, :], (LANES, AT_HEAD_DIM)).astype(BF16)
        s = _dot_nt(kbuf[slot, hd].astype(BF16), q_rep) * scale
        s_own = s_new[hd:hd + 1, :]
        m = jnp.maximum(jnp.max(s, axis=0, keepdims=True), s_own)
        p = jnp.exp(s - m)
        p_new = jnp.exp(s_own - m)
        l = jnp.sum(p, axis=0, keepdims=True) + p_new
        o = (jnp.sum(p * vbuf[slot, hd], axis=0, keepdims=True) + p_new * vn[hd:hd + 1, :]) / l
        out = jnp.where(row == hd, o, out)
    o_ref[0] = out


def _gather_attend(page_table_flat, sel_flat, q3, kn3, vn3, cache_k, cache_v, *, layer, n_pages):
    db = q3.shape[0]
    rows = MOBA_TOPK * MOBA_BLOCK
    head_spec = pl.BlockSpec((1, AT_HEADS, AT_HEAD_DIM), lambda b, pt, sel: (b, 0, 0))
    any_spec = pl.BlockSpec(memory_space=pl.ANY)
    return pl.pallas_call(
        functools.partial(_gather_attend_kernel, layer=layer, n_pages=n_pages),
        out_shape=jax.ShapeDtypeStruct((db, AT_HEADS, AT_HEAD_DIM), F32),
        grid_spec=pltpu.PrefetchScalarGridSpec(
            num_scalar_prefetch=2,
            grid=(db,),
            in_specs=[head_spec, head_spec, head_spec, any_spec, any_spec],
            out_specs=head_spec,
            scratch_shapes=[pltpu.VMEM((2, AT_HEADS, rows, AT_HEAD_DIM), F32),
                            pltpu.VMEM((2, AT_HEADS, rows, AT_HEAD_DIM), F32),
                            pltpu.SemaphoreType.DMA((2, 2))]),
        compiler_params=_params("arbitrary"),
        name="gather_attend",
    )(page_table_flat, sel_flat, q3, kn3, vn3, cache_k, cache_v)


def _moba_sample(qb, kb, vb, cache_k, cache_v, page_table, cursor, block_sums, *, layer):
    db = qb.shape[0]
    n_pages = PAST_LEN // PAGE_SIZE
    n_blocks = PAST_LEN // MOBA_BLOCK
    assert PAST_LEN % MOBA_BLOCK == 0
    rest = db * n_pages - cursor.next_page
    assert rest % PAGES_PER_STEP == 0
    if rest:
        _, tail = _tiled_call(_no_body, (), name="block_sums", steps=rest // PAGES_PER_STEP,
                              in_specs=[], out_specs=[], out_shape=[],
                              stream=cursor.take(rest // PAGES_PER_STEP, PAGES_PER_STEP))
        block_sums = block_sums + [tail]
    ksum = jnp.concatenate(block_sums, axis=0).reshape(db, n_blocks, AT_HEADS, AT_HEAD_DIM)

    heads = lambda a: a.reshape(db, AT_HEADS, AT_HEAD_DIM)
    q3 = heads(qb)
    sel = _gate_topk(q3, ksum)
    sel_flat = sel[:, :, :MOBA_TOPK].reshape(-1)
    out = _gather_attend(page_table.reshape(-1), sel_flat, q3, heads(kb), heads(vb), cache_k, cache_v,
                         layer=layer, n_pages=n_pages)
    return out.reshape(db, AT_WIDTH)


def _mix_kernel(ha_ref, hb_ref, h1_ref, wg_ref, wa_ref, wb_ref, wo_ref, g_ref, b_ref, o_ref, *, alpha):
    h1 = h1_ref[...]
    gates = _dot(h1.astype(BF16), wg_ref[...])
    a = _dot(ha_ref[...], wa_ref[...])
    bm = _dot(hb_ref[...], wb_ref[...])
    inner = jax.nn.sigmoid(gates[:, :D_MODEL]) * a + jax.nn.sigmoid(gates[:, D_MODEL:]) * bm
    mix = _dot(inner.astype(BF16), wo_ref[...])
    o_ref[...] = _layer_norm(alpha * h1 + mix, g_ref[...], b_ref[...])


def _mix(ha, hb, h1, wg, wa, wb, wo, g, b, *, alpha, tm, stream=None):
    n, d = h1.shape
    row = pl.BlockSpec((tm, d), lambda i, *_: (i, 0))
    (h2,), ksum = _tiled_call(
        functools.partial(_mix_kernel, alpha=alpha), (ha, hb, h1, wg, wa, wb, wo, g, b),
        name="mix", steps=n // tm,
        in_specs=[row, row, row, _const_spec(wg.shape), _const_spec(wa.shape), _const_spec(wb.shape),
                  _const_spec(wo.shape), _const_spec((1, d)), _const_spec((1, d))],
        out_specs=[row], out_shape=[jax.ShapeDtypeStruct((n, d), F32)], stream=stream)
    return h2, ksum


def _rope_tables(pos):
    half = AT_HEAD_DIM // 2
    inv = ROPE_THETA ** (-jnp.arange(half, dtype=F32) / half)
    ang = pos.astype(F32)[:, None] * inv[None, :]
    cos, sin = jnp.cos(ang), jnp.sin(ang)
    return jnp.concatenate([cos, cos], -1), jnp.concatenate([-sin, sin], -1)


def _layer_weights(l, ln_g, ln_b, ffn1_up, ffn1_down, ffn2_up, ffn2_down, w_in, b_if, mh_g, w_a, w_b, w_o):
    w = w_in[l]
    w_if = jnp.pad(w[:, OFF_IF:OFF_QB], ((0, 0), (0, LANES - 2 * ML_HEADS)))
    wif_hi, wif_lo = _split2(w_if)
    vec = lambda a: a.reshape(1, -1)
    return dict(
        ln_g=[vec(ln_g[l, i]) for i in range(3)], ln_b=[vec(ln_b[l, i]) for i in range(3)],
        up1=ffn1_up[l].astype(BF16), dn1=ffn1_down[l].astype(BF16),
        up2=ffn2_up[l].astype(BF16), dn2=ffn2_down[l].astype(BF16),
        w_main=jnp.concatenate([w[:, :OFF_IF], w[:, OFF_QB:OFF_G]], axis=1).astype(BF16),
        wif_hi=wif_hi, wif_lo=wif_lo,
        bif=jnp.pad(b_if[l], (0, LANES - 2 * ML_HEADS)).reshape(1, LANES),
        w_g=w[:, OFF_G:].astype(BF16), mhg=vec(mh_g[l]),
        w_a=w_a[l].astype(BF16), w_b=w_b[l].astype(BF16), w_o=w_o[l].astype(BF16))


def _decoder_layer(x, rope, mlstm_fn, moba_fn, w, *, alpha, tm, cursor=None):
    cos, sin = rope
    n = x.shape[0]
    tm_proj = min(tm, 256)
    take = lambda steps, pages: cursor.take(steps, pages) if cursor is not None else None
    h1, s1 = _ffn_ln(x, w["up1"], w["dn1"], w["ln_g"][0], w["ln_b"][0], alpha=alpha, tm=tm,
                     stream=take(n // tm, PAGES_PER_DENSE_STEP))
    (qa, ka, va, oa, gif, qb, kb, vb), s2 = _proj(h1, w["w_main"], w["wif_hi"], w["wif_lo"], w["bif"],
                                                  cos, sin, tm=tm_proj, stream=take(n // tm_proj, PAGES_PER_PROJ_STEP))
    (ha, c, n_state, m), s3 = mlstm_fn(qa, ka, va, oa, gif, w["mhg"])
    hb, s4 = moba_fn(qb, kb, vb)
    h2, s5 = _mix(ha.astype(BF16), hb.astype(BF16), h1, w["w_g"], w["w_a"], w["w_b"], w["w_o"],
                  w["ln_g"][1], w["ln_b"][1], alpha=alpha, tm=tm, stream=take(n // tm, PAGES_PER_DENSE_STEP))
    y, s6 = _ffn_ln(h2, w["up2"], w["dn2"], w["ln_g"][2], w["ln_b"][2], alpha=alpha, tm=tm,
                    stream=take(n // tm, PAGES_PER_DENSE_STEP))
    return y, kb, vb, c, n_state, m, [s for s in (s1, s2, s3, s4, s5, s6) if s is not None]


def kernel(x_prompt, x_sample, cache_k, cache_v, state_mlstm_C, state_mlstm_n, state_mlstm_m, page_table,
           ln_g, ln_b, ffn1_up, ffn1_down, ffn2_up, ffn2_down, w_in, b_if, mh_g, w_a, w_b, w_o):
    batch, seq, d = x_prompt.shape
    db, t_new, _ = x_sample.shape
    depth = ln_g.shape[0]
    assert t_new == 1 and seq % ML_CHUNK == 0
    alpha = (2 * depth) ** 0.25

    rope_p = _rope_tables(jnp.arange(seq, dtype=jnp.int32))
    cos_s, sin_s = _rope_tables(PAST_LEN + jnp.arange(t_new, dtype=jnp.int32))
    rope_s = (jnp.tile(cos_s, (db, 1)), jnp.tile(sin_s, (db, 1)))

    yp = x_prompt.reshape(batch * seq, d)
    ys = x_sample.reshape(db * t_new, d)
    outs_p, outs_s = [], []
    for l in range(depth):
        w = _layer_weights(l, ln_g, ln_b, ffn1_up, ffn1_down, ffn2_up, ffn2_down, w_in, b_if, mh_g, w_a, w_b, w_o)

        cursor = _PageCursor(page_table, cache_k, l)
        yp, k1, v1, c1, n1, m1, block_sums = _decoder_layer(
            yp, rope_p,
            functools.partial(_mlstm_chunkwise, batch=batch, seq=seq, cursor=cursor),
            functools.partial(_moba_prompt, batch=batch, seq=seq, cursor=cursor),
            w, alpha=alpha, tm=512, cursor=cursor)
        outs_p.append((k1.reshape(batch, seq, AT_HEADS, AT_HEAD_DIM), v1.reshape(batch, seq, AT_HEADS, AT_HEAD_DIM),
                       c1, n1, m1[:, 0, :ML_HEADS]))

        ys, k2, v2, c2, n2, m2, _ = _decoder_layer(
            ys, rope_s,
            lambda qa, ka, va, oa, gif, mhg: (_mlstm_step(
                qa, ka, va, oa, gif, mhg, state_mlstm_C[l], state_mlstm_n[l], state_mlstm_m[l]), None),
            lambda qb, kb, vb: (_moba_sample(qb, kb, vb, cache_k, cache_v, page_table, cursor, block_sums,
                                             layer=l), None),
            w, alpha=alpha, tm=db * t_new)
        outs_s.append((k2.reshape(db, t_new, AT_HEADS, AT_HEAD_DIM), v2.reshape(db, t_new, AT_HEADS, AT_HEAD_DIM),
                       c2, n2, m2))

    kp, vp, cp, np_, mp = (jnp.stack(a) for a in zip(*outs_p))
    ks, vs, cs, ns, ms = (jnp.stack(a) for a in zip(*outs_s))
    return (yp.reshape(batch, seq, d), ys.reshape(db, t_new, d), kp, vp, cp, np_, mp, ks, vs, cs, ns, ms)
```
